```python
import math
import jax, jax.numpy as jnp
from jax import lax
import numpy as np

D_MODEL = 4096
BATCH = 2
SEQ = 8192
DEPTH = 2

D_FF = 11008
FFN_RESIDUAL = 0.5
HEAD_DIM = 128
HEADS_A = 24
A_BRANCHES = ((128, 1), (512, 4), (2048, 16))
FNET_GROUPS = 8
FNET_GROUP_DIM = 128
QKV_DIM = HEADS_A * HEAD_DIM
FNET_DIM = FNET_GROUPS * FNET_GROUP_DIM
IN_AB = 3 * QKV_DIM + FNET_DIM
MIX_AB = QKV_DIM + FNET_DIM
REL_BUCKETS = 32
REL_MAX_DISTANCE = 1024
CONV_C_DIM = 2048
CONV_C_WIDTH = 31
CONV_D_DIM = 2048
CONV_D_WIDTH = 3
IN_CD = 2 * CONV_C_DIM + 3 * CONV_D_DIM
MIX_CD = CONV_C_DIM + CONV_D_DIM
N_EVEN = (DEPTH + 1) // 2
N_ODD = DEPTH // 2
NORM_EPS = 1e-6
NEG_INF = -1e30

kernel_name = "hybrid_dilated_fourier_conv_encoder"


def rms_norm(x, g):
    xf = x.astype(jnp.float32)
    y = xf * lax.rsqrt(jnp.mean(xf * xf, axis=-1, keepdims=True) + NORM_EPS)
    return (y * g.astype(jnp.float32)).astype(x.dtype)


def layer_norm(x, g, b):
    xf = x.astype(jnp.float32)
    mu = jnp.mean(xf, axis=-1, keepdims=True)
    xc = xf - mu
    y = xc * lax.rsqrt(jnp.mean(xc * xc, axis=-1, keepdims=True) + NORM_EPS)
    return (y * g.astype(jnp.float32) + b.astype(jnp.float32)).astype(x.dtype)


def swiglu(h, w_gate, w_up, w_down):
    return (jax.nn.silu(h @ w_gate) * (h @ w_up)) @ w_down


def depthwise_conv(x, w):
    width, ch = w.shape
    pad = (width - 1) // 2
    return lax.conv_general_dilated(
        x, w[:, None, :].astype(x.dtype), window_strides=(1,), padding=[(pad, pad)],
        dimension_numbers=("NWC", "WIO", "NWC"), feature_group_count=ch)


def t5_bucket(rel):
    half = REL_BUCKETS // 2
    max_exact = half // 2
    ret = jnp.where(rel > 0, half, 0)
    n = jnp.abs(rel)
    nf = jnp.maximum(n, 1).astype(jnp.float32)
    large = max_exact + (jnp.log(nf / max_exact) / math.log(REL_MAX_DISTANCE / max_exact)
                         * (half - max_exact)).astype(jnp.int32)
    large = jnp.minimum(large, half - 1)
    return ret + jnp.where(n < max_exact, n, large)


def dilated_branch(q, k, v, rel_bias, window, dilation):
    B, S, H, E = q.shape
    d = dilation
    K = window // (2 * dilation)
    L = S // d
    nb = -(-L // K)
    Lp = nb * K

    def strided(t):
        return t.reshape(B, L, d, H, E)

    qb = jnp.pad(strided(q), ((0, 0), (0, Lp - L), (0, 0), (0, 0), (0, 0))).reshape(B, nb, K, d, H, E)

    def key_blocks(t):
        tp = jnp.pad(strided(t), ((0, 0), (K, Lp - L + K), (0, 0), (0, 0), (0, 0)))
        return jnp.concatenate(
            [tp[:, j * K: j * K + Lp].reshape(B, nb, K, d, H, E) for j in range(3)], axis=2)

    kb = key_blocks(k)
    vb = key_blocks(v)
    a_idx = jnp.arange(K)[:, None]
    c_idx = jnp.arange(3 * K)[None, :]
    rel = c_idx - K - a_idx
    band = jnp.abs(rel) <= K
    n_key = jnp.arange(nb)[:, None] * K + jnp.arange(3 * K)[None, :] - K
    key_ok = (n_key >= 0) & (n_key < L)
    mask = band[None, :, :] & key_ok[:, None, :]
    bias = jnp.transpose(rel_bias[t5_bucket(rel * d)], (2, 0, 1)).astype(jnp.float32)
    scale = 1.0 / math.sqrt(E)
    s = jnp.einsum("bnqrhe,bnkrhe->bnrhqk", qb, kb, preferred_element_type=jnp.float32) * scale + bias
    s = jnp.where(mask[None, :, None, None], s, NEG_INF)
    lse = jax.nn.logsumexp(s, axis=-1)
    p = jnp.exp(s - lse[..., None])
    o = jnp.einsum("bnrhqk,bnkrhe->bnqrhe", p.astype(v.dtype), vb)
    o = o.reshape(B, Lp, d, H, E)[:, :L].reshape(B, S, H, E)
    lse = jnp.transpose(lse, (0, 1, 4, 2, 3)).reshape(B, Lp, d, H)[:, :L].reshape(B, S, H)
    return o, lse


def dilated_mixture(q, k, v, rel_bias):
    outs, lses = [], []
    for window, dilation in A_BRANCHES:
        o, l = dilated_branch(q, k, v, rel_bias, window, dilation)
        outs.append(o)
        lses.append(l)
    wts = jax.nn.softmax(jnp.stack(lses, axis=0), axis=0)
    return jnp.einsum("gbsh,gbshe->bshe", wts.astype(q.dtype), jnp.stack(outs, axis=0))


def fourier_mix(u):
    B, S, _ = u.shape
    ug = u.reshape(B, S, FNET_GROUPS, FNET_GROUP_DIM).astype(jnp.float32)
    f = jnp.fft.fft2(ug, axes=(1, 3), norm="ortho").real
    return f.reshape(B, S, FNET_DIM).astype(u.dtype)


def mixer_ab(h, w_in, w_out, rel_bias):
    B, S, _ = h.shape
    proj = h @ w_in
    q, k, v, u = jnp.split(proj, [QKV_DIM, 2 * QKV_DIM, 3 * QKV_DIM], axis=-1)
    q = q.reshape(B, S, HEADS_A, HEAD_DIM)
    k = k.reshape(B, S, HEADS_A, HEAD_DIM)
    v = v.reshape(B, S, HEADS_A, HEAD_DIM)
    a_out = dilated_mixture(q, k, v, rel_bias).reshape(B, S, QKV_DIM)
    b_out = fourier_mix(u)
    return jnp.concatenate([a_out, b_out], axis=-1) @ w_out


def mixer_cd(h, w_in, conv_c_w, conv_c_b, ln_c_g, ln_c_b, conv_d_w, w_out):
    proj = h @ w_in
    c_val, c_gate, d_b, d_c, d_h = jnp.split(
        proj, [CONV_C_DIM, 2 * CONV_C_DIM, 2 * CONV_C_DIM + CONV_D_DIM,
               2 * CONV_C_DIM + 2 * CONV_D_DIM], axis=-1)
    c = c_val * jax.nn.sigmoid(c_gate)
    c = depthwise_conv(c, conv_c_w) + conv_c_b
    c = jax.nn.silu(layer_norm(c, ln_c_g, ln_c_b))
    d = d_b * depthwise_conv(d_c * d_h, conv_d_w)
    return jnp.concatenate([c, d], axis=-1) @ w_out


def setup_inputs(seed: int = 0) -> dict:
    key = jax.random.key(seed)
    ks = jax.random.split(key, 24)
    f32 = jnp.float32

    def nrm(k, shape, scale):
        return jax.random.normal(k, shape, f32) * scale

    def gain(k, shape):
        return 1.0 + 0.02 * jax.random.normal(k, shape, f32)

    return {
        "x": jax.random.normal(ks[0], (BATCH, SEQ, D_MODEL), f32),
        "ffn1_norm": gain(ks[1], (DEPTH, D_MODEL)),
        "ffn1_w_gate": nrm(ks[2], (DEPTH, D_MODEL, D_FF), D_MODEL ** -0.5),
        "ffn1_w_up": nrm(ks[3], (DEPTH, D_MODEL, D_FF), D_MODEL ** -0.5),
        "ffn1_w_down": nrm(ks[4], (DEPTH, D_FF, D_MODEL), D_FF ** -0.5),
        "mix_norm": gain(ks[5], (DEPTH, D_MODEL)),
        "ffn2_norm": gain(ks[6], (DEPTH, D_MODEL)),
        "ffn2_w_gate": nrm(ks[7], (DEPTH, D_MODEL, D_FF), D_MODEL ** -0.5),
        "ffn2_w_up": nrm(ks[8], (DEPTH, D_MODEL, D_FF), D_MODEL ** -0.5),
        "ffn2_w_down": nrm(ks[9], (DEPTH, D_FF, D_MODEL), D_FF ** -0.5),
        "rel_bias": nrm(ks[10], (REL_BUCKETS, HEADS_A), 0.5),
        "w_in_ab": nrm(ks[11], (N_EVEN, D_MODEL, IN_AB), D_MODEL ** -0.5),
        "w_out_ab": nrm(ks[12], (N_EVEN, MIX_AB, D_MODEL), MIX_AB ** -0.5),
        "w_in_cd": nrm(ks[13], (N_ODD, D_MODEL, IN_CD), D_MODEL ** -0.5),
        "conv_c_w": nrm(ks[14], (N_ODD, CONV_C_WIDTH, CONV_C_DIM), CONV_C_WIDTH ** -0.5),
        "conv_c_b": nrm(ks[15], (N_ODD, CONV_C_DIM), 0.02),
        "ln_c_g": gain(ks[16], (N_ODD, CONV_C_DIM)),
        "ln_c_b": nrm(ks[17], (N_ODD, CONV_C_DIM), 0.02),
        "conv_d_w": nrm(ks[18], (N_ODD, CONV_D_WIDTH, CONV_D_DIM), CONV_D_WIDTH ** -0.5),
        "w_out_cd": nrm(ks[19], (N_ODD, MIX_CD, D_MODEL), MIX_CD ** -0.5),
        "final_norm": gain(ks[20], (D_MODEL,)),
    }


def reference(x, ffn1_norm, ffn1_w_gate, ffn1_w_up, ffn1_w_down, mix_norm,
              ffn2_norm, ffn2_w_gate, ffn2_w_up, ffn2_w_down, rel_bias,
              w_in_ab, w_out_ab, w_in_cd, conv_c_w, conv_c_b, ln_c_g, ln_c_b,
              conv_d_w, w_out_cd, final_norm):
    for layer in range(DEPTH):
        x = x + FFN_RESIDUAL * swiglu(rms_norm(x, ffn1_norm[layer]), ffn1_w_gate[layer],
                                      ffn1_w_up[layer], ffn1_w_down[layer])
        h = rms_norm(x, mix_norm[layer])
        if layer % 2 == 0:
            i = layer // 2
            x = x + mixer_ab(h, w_in_ab[i], w_out_ab[i], rel_bias)
        else:
            i = layer // 2
            x = x + mixer_cd(h, w_in_cd[i], conv_c_w[i], conv_c_b[i], ln_c_g[i], ln_c_b[i],
                             conv_d_w[i], w_out_cd[i])
        x = x + FFN_RESIDUAL * swiglu(rms_norm(x, ffn2_norm[layer]), ffn2_w_gate[layer],
                                      ffn2_w_up[layer], ffn2_w_down[layer])
    return rms_norm(x, final_norm)
```

```python
import functools
import math

import jax
import jax.numpy as jnp
import numpy as np
from jax import lax
from jax.experimental import pallas as pl
from jax.experimental.pallas import tpu as pltpu

HEAD_DIM = 128
HEADS_A = 24
A_BRANCHES = ((128, 1), (512, 4), (2048, 16))
FNET_GROUPS = 8
FNET_GROUP_DIM = 128
REL_BUCKETS = 32
REL_MAX_DISTANCE = 1024
CONV_C_DIM = 2048
CONV_D_DIM = 2048
FFN_RESIDUAL = 0.5
NORM_EPS = 1e-6
NEG_INF = -1e30

LANES = 128
ATT_TQ = 128
ATT_HALO = 64
FFN_PAD = 1024
MIB = 1024 * 1024

_F32 = jnp.float32
_BF16 = jnp.bfloat16


def _params(sem, vmem_mib):
    return pltpu.CompilerParams(dimension_semantics=sem, vmem_limit_bytes=vmem_mib * MIB)


def _pick(n, prefs):
    for p in prefs:
        if n % p == 0:
            return p
    return n


def _norm_matmul_kernel(x_ref, g_ref, *refs, swiglu):
    n_w = 2 if swiglu else 1
    w_refs, o_ref, h_ref = refs[:n_w], refs[n_w], refs[n_w + 1]

    @pl.when(pl.program_id(1) == 0)
    def _():
        x = x_ref[...]
        inv = lax.rsqrt(jnp.mean(x * x, axis=-1, keepdims=True) + NORM_EPS)
        h_ref[...] = (x * inv * g_ref[...]).astype(h_ref.dtype)

    h = h_ref[...]
    a = jnp.dot(h, w_refs[0][...], preferred_element_type=_F32)
    if swiglu:
        b = jnp.dot(h, w_refs[1][...], preferred_element_type=_F32)
        a = a * jax.nn.sigmoid(a) * b
    o_ref[...] = a.astype(o_ref.dtype)


def _norm_matmul(x, g, ws, out_dtype, name):
    m, d = x.shape
    n = ws[0].shape[1]
    tm = _pick(m, (512, 256, 128))
    tn = _pick(n, (512, 256, 128))
    swiglu = len(ws) == 2
    w_spec = pl.BlockSpec((d, tn), lambda i, j: (0, j))
    return pl.pallas_call(
        functools.partial(_norm_matmul_kernel, swiglu=swiglu),
        grid=(m // tm, n // tn),
        in_specs=[pl.BlockSpec((tm, d), lambda i, j: (i, 0)),
                  pl.BlockSpec((1, d), lambda i, j: (0, 0))] + [w_spec] * len(ws),
        out_specs=pl.BlockSpec((tm, tn), lambda i, j: (i, j)),
        out_shape=jax.ShapeDtypeStruct((m, n), out_dtype),
        scratch_shapes=[pltpu.VMEM((tm, d), _BF16)],
        compiler_params=_params(("parallel", "arbitrary"), 52),
        name=name,
    )(x, g.reshape(1, d), *ws)


def _matmul_resid_kernel(a_ref, w_ref, x_ref, o_ref, acc_ref, *, scale, nk):
    p = jnp.dot(a_ref[...], w_ref[...], preferred_element_type=_F32)
    if nk == 1:
        o_ref[...] = x_ref[...] + scale * p
        return
    k = pl.program_id(2)

    @pl.when(k == 0)
    def _():
        acc_ref[...] = p

    @pl.when(jnp.logical_and(k > 0, k < nk - 1))
    def _():
        acc_ref[...] += p

    @pl.when(k == nk - 1)
    def _():
        o_ref[...] = x_ref[...] + scale * (acc_ref[...] + p)


def _matmul_resid(a, w, x, scale, name):
    m, kdim = a.shape
    n = w.shape[1]
    tm = _pick(m, (1024, 512, 256, 128))
    tn = _pick(n, (1024, 512, 256, 128))
    tk = _pick(kdim, (2816, 2048, 1024, 512, 256, 128))
    nk = kdim // tk
    return pl.pallas_call(
        functools.partial(_matmul_resid_kernel, scale=scale, nk=nk),
        grid=(m // tm, n // tn, nk),
        in_specs=[pl.BlockSpec((tm, tk), lambda i, j, k: (i, k)),
                  pl.BlockSpec((tk, tn), lambda i, j, k: (k, j)),
                  pl.BlockSpec((tm, tn), lambda i, j, k: (i, j))],
        out_specs=pl.BlockSpec((tm, tn), lambda i, j, k: (i, j)),
        out_shape=jax.ShapeDtypeStruct((m, n), _F32),
        scratch_shapes=[pltpu.VMEM((tm, tn), _F32)],
        compiler_params=_params(("parallel", "parallel", "arbitrary"), 52),
        name=name,
    )(a, w, x)


def _ffn(x, g, w_gate, w_up, w_down, name):
    f = w_gate.shape[1]
    fp = -(-f // FFN_PAD) * FFN_PAD
    wg = jnp.pad(w_gate.astype(_BF16), ((0, 0), (0, fp - f)))
    wu = jnp.pad(w_up.astype(_BF16), ((0, 0), (0, fp - f)))
    wd = jnp.pad(w_down.astype(_BF16), ((0, fp - f), (0, 0)))
    hidden = _norm_matmul(x, g, (wg, wu), _BF16, name + "_up")
    return _matmul_resid(hidden, wd, x, FFN_RESIDUAL, name + "_down")


def _t5_bucket_np(rel):
    half = REL_BUCKETS // 2
    max_exact = half // 2
    ret = np.where(rel > 0, half, 0)
    n = np.abs(rel)
    nf = np.maximum(n, 1).astype(np.float64)
    val = np.log(nf / max_exact) / math.log(REL_MAX_DISTANCE / max_exact) * (half - max_exact)
    large = np.minimum(max_exact + val.astype(np.int64), half - 1)
    return (ret + np.where(n < max_exact, n, large)).astype(np.int32)


def _attn_bias_tiles(rel_bias, dilation):
    tq, halo = ATT_TQ, ATT_HALO
    a = np.arange(tq)[:, None]
    c = np.arange(tq + 2 * halo)[None, :]
    rel = c - halo - a
    band = np.abs(rel) <= halo
    bias = jnp.transpose(rel_bias[_t5_bucket_np(rel * dilation)], (2, 0, 1)).astype(_F32)
    ok_first = np.broadcast_to(c >= halo, band.shape)
    ok_last = np.broadcast_to(c < tq + halo, band.shape)
    masks = [band, band & ok_first, band & ok_last, band & ok_first & ok_last]
    return jnp.stack([jnp.where(mk[None], bias, NEG_INF) for mk in masks], axis=1)


def _attn_kernel(bm_ref, q_ref, k_ref, v_ref, o_ref, lse_ref, kpad, vpad, *, scale):
    seq = q_ref.shape[1]
    tq, halo = ATT_TQ, ATT_HALO
    nt = seq // tq
    zeros = jnp.zeros((halo, HEAD_DIM), kpad.dtype)
    for pad, src in ((kpad, k_ref), (vpad, v_ref)):
        pad[0:halo, :] = zeros
        pad[halo:halo + seq, :] = src[0]
        pad[halo + seq:, :] = zeros

    def tile(t, carry):
        m0 = pl.multiple_of(t * tq, tq)
        q = q_ref[0, pl.ds(m0, tq), :]
        kw = kpad[pl.ds(m0, tq + 2 * halo), :]
        vw = vpad[pl.ds(m0, tq + 2 * halo), :]
        s = lax.dot_general(q, kw, (((1,), (1,)), ((), ())), preferred_element_type=_F32)
        variant = (t == 0).astype(jnp.int32) + 2 * (t == nt - 1).astype(jnp.int32)
        s = s * scale + bm_ref[0, variant]
        mx = jnp.max(s, axis=-1, keepdims=True)
        p = jnp.exp(s - mx)
        den = jnp.sum(p, axis=-1, keepdims=True)
        o = jnp.dot(p.astype(vw.dtype), vw, preferred_element_type=_F32) / den
        o_ref[0, pl.ds(m0, tq), :] = o.astype(o_ref.dtype)
        lse = mx + jnp.log(den)
        lse_row = jnp.transpose(jnp.broadcast_to(lse, (tq, LANES)))[0:1, :]
        lse_ref[0, 0, 0, pl.ds(t, 1), :] = lse_row
        return carry

    lax.fori_loop(0, nt, tile, 0)


def _attn_branch(qkv, bias_tiles, dilation, name):
    b, s, _ = qkv.shape
    h, e, d = HEADS_A, HEAD_DIM, dilation
    sub = s // d
    assert sub % ATT_TQ == 0
    qkv_v = qkv.reshape(b, sub, d * 3 * h * e)

    def in_spec(which):
        return pl.BlockSpec((1, sub, e), lambda hi, bi, ri: (bi, 0, ri * 3 * h + which * h + hi))

    out, lse = pl.pallas_call(
        functools.partial(_attn_kernel, scale=1.0 / math.sqrt(e)),
        grid=(h, b, d),
        in_specs=[pl.BlockSpec((1, 4, ATT_TQ, ATT_TQ + 2 * ATT_HALO), lambda hi, bi, ri: (hi, 0, 0, 0)),
                  in_spec(0), in_spec(1), in_spec(2)],
        out_specs=[pl.BlockSpec((1, sub, e), lambda hi, bi, ri: (bi, 0, ri * h + hi)),
                   pl.BlockSpec((1, 1, 1, sub // ATT_TQ, ATT_TQ), lambda hi, bi, ri: (hi, bi, ri, 0, 0))],
        out_shape=[jax.ShapeDtypeStruct((b, sub, d * h * e), _BF16),
                   jax.ShapeDtypeStruct((h, b, d, sub // ATT_TQ, ATT_TQ), _F32)],
        scratch_shapes=[pltpu.VMEM((sub + 2 * ATT_HALO, e), _BF16),
                        pltpu.VMEM((sub + 2 * ATT_HALO, e), _BF16)],
        compiler_params=_params(("parallel", "parallel", "parallel"), 32),
        name=name,
    )(bias_tiles, qkv_v, qkv_v, qkv_v)
    lse = jnp.transpose(lse.reshape(h, b, d, sub), (1, 3, 2, 0)).reshape(b, s // LANES, LANES, h)
    return out.reshape(b, s, h * e), jnp.transpose(lse, (0, 1, 3, 2))


def _combine_kernel(l1_ref, l2_ref, l3_ref, o1_ref, o2_ref, o3_ref, out_ref):
    l_refs, o_refs = (l1_ref, l2_ref, l3_ref), (o1_ref, o2_ref, o3_ref)
    heads = l1_ref.shape[2]
    for j in range(l1_ref.shape[1]):
        ls = [r[0, j] for r in l_refs]
        mx = jnp.maximum(jnp.maximum(ls[0], ls[1]), ls[2])
        es = [jnp.exp(l - mx) for l in ls]
        den = es[0] + es[1] + es[2]
        fill = jnp.zeros((LANES - heads, LANES), _F32)
        wt = [jnp.transpose(jnp.concatenate([ex / den, fill], axis=0)) for ex in es]
        rows = slice(j * LANES, (j + 1) * LANES)
        for hh in range(heads):
            cols = slice(hh * HEAD_DIM, (hh + 1) * HEAD_DIM)
            acc = wt[0][:, hh:hh + 1] * o_refs[0][0, rows, cols].astype(_F32)
            for g in (1, 2):
                acc = acc + wt[g][:, hh:hh + 1] * o_refs[g][0, rows, cols].astype(_F32)
            out_ref[0, rows, cols] = acc.astype(out_ref.dtype)


def _combine(outs, lses):
    b, s, width = outs[0].shape
    ts = _pick(s, (256, 128))
    heads = lses[0].shape[2]
    l_spec = pl.BlockSpec((1, ts // LANES, heads, LANES), lambda bi, i: (bi, i, 0, 0))
    o_spec = pl.BlockSpec((1, ts, width), lambda bi, i: (bi, i, 0))
    return pl.pallas_call(
        _combine_kernel,
        grid=(b, s // ts),
        in_specs=[l_spec] * 3 + [o_spec] * 3,
        out_specs=o_spec,
        out_shape=jax.ShapeDtypeStruct((b, s, width), _BF16),
        compiler_params=_params(("parallel", "parallel"), 32),
        name="attn_combine",
    )(*lses, *outs)


def _dilated_mixture(qkv, rel_bias):
    assert len(A_BRANCHES) == 3
    outs, lses = [], []
    for window, dilation in A_BRANCHES:
        assert window // (2 * dilation) == ATT_HALO
        o, l = _attn_branch(qkv, _attn_bias_tiles(rel_bias, dilation), dilation, f"attn_d{dilation}")
        outs.append(o)
        lses.append(l)
    return _combine(outs, lses)


_HI = lax.Precision.HIGHEST


def _fft_stage1_kernel(f_ref, u_ref, ar_ref, ai_ref):
    n1 = u_ref.shape[1]
    r = jnp.dot(f_ref[...], u_ref[0], precision=_HI, preferred_element_type=_F32)
    ar_ref[0] = r[:n1]
    ai_ref[0] = r[n1:]


def _fft_stage2_kernel(twc_ref, tws_ref, c2_ref, s2_ref, cc_ref, sc_ref, ar_ref, ai_ref, o_ref):
    dot = functools.partial(jnp.dot, precision=_HI, preferred_element_type=_F32)
    ar, ai = ar_ref[0], ai_ref[0]
    c, s = twc_ref[...], tws_ref[...]
    pr = ar * c + ai * s
    pi = ai * c - ar * s
    c2, s2 = c2_ref[...], s2_ref[...]
    zr = dot(c2, pr) + dot(s2, pi)
    zi = dot(c2, pi) - dot(s2, pr)
    gd = cc_ref.shape[0]
    for g in range(ar.shape[1] // gd):
        cols = slice(g * gd, (g + 1) * gd)
        f = dot(zr[:, cols], cc_ref[...]) + dot(zi[:, cols], sc_ref[...])
        o_ref[0, :, cols] = f.astype(o_ref.dtype)


def _fourier_mix(u):
    b, s, width = u.shape
    gd = FNET_GROUP_DIM
    n2 = 128
    n1 = s // n2
    assert n1 * n2 == s and n1 % 8 == 0

    def trig(n, rows, cols):
        ang = 2.0 * np.pi * ((np.arange(rows)[:, None] * np.arange(cols)[None, :]) % n) / n
        return np.cos(ang), np.sin(ang)

    c1, s1 = trig(n1, n1, n1)
    f1 = jnp.asarray(np.concatenate([c1, -s1], axis=0), _F32)
    tc, tsn = trig(s, n1, n2)
    twc = jnp.asarray(tc.reshape(s, 1), _F32)
    tws = jnp.asarray(tsn.reshape(s, 1), _F32)
    c2, s2 = trig(n2, n2, n2)
    norm = 1.0 / math.sqrt(s * gd)
    cc, sc = trig(gd, gd, gd)
    c2, s2 = jnp.asarray(c2, _F32), jnp.asarray(s2, _F32)
    cc, sc = jnp.asarray(cc * norm, _F32), jnp.asarray(sc * norm, _F32)

    lanes = n2 * width
    tl = _pick(lanes, (4096, 2048, 1024, 512, 256, 128))
    a_spec = pl.BlockSpec((1, n1, tl), lambda bi, j: (bi, 0, j))
    ar, ai = pl.pallas_call(
        _fft_stage1_kernel,
        grid=(b, lanes // tl),
        in_specs=[pl.BlockSpec((2 * n1, n1), lambda bi, j: (0, 0)), a_spec],
        out_specs=[a_spec, a_spec],
        out_shape=[jax.ShapeDtypeStruct((b, n1, lanes), _F32)] * 2,
        compiler_params=_params(("parallel", "parallel"), 32),
        name="fft_stage1",
    )(f1, u.reshape(b, n1, lanes))

    tw_spec = pl.BlockSpec((n2, 1), lambda bi, k1: (k1, 0))
    m_spec = pl.BlockSpec((n2, n2), lambda bi, k1: (0, 0))
    g_spec = pl.BlockSpec((gd, gd), lambda bi, k1: (0, 0))
    z_spec = pl.BlockSpec((1, n2, width), lambda bi, k1: (bi, k1, 0))
    out = pl.pallas_call(
        _fft_stage2_kernel,
        grid=(b, n1),
        in_specs=[tw_spec, tw_spec, m_spec, m_spec, g_spec, g_spec, z_spec, z_spec],
        out_specs=pl.BlockSpec((1, n2, width), lambda bi, k1: (bi, 0, k1)),
        out_shape=jax.ShapeDtypeStruct((b, n2, n1 * width), _BF16),
        compiler_params=_params(("parallel", "parallel"), 32),
        name="fft_stage2",
    )(twc, tws, c2, s2, cc, sc, ar.reshape(b, s, width), ai.reshape(b, s, width))
    return out.reshape(b, s, width)


CONV_HALO = 16
CONV_ROWS = 64


def _glu_conv_kernel(w_ref, b_ref, vc_ref, vp_ref, vn_ref, gc_ref, gp_ref, gn_ref, y_ref, ext, *, width):
    i, last = pl.program_id(1), pl.num_programs(1) - 1
    ts, tc = vc_ref.shape[1], vc_ref.shape[2]
    halo, pad = CONV_HALO, (width - 1) // 2

    def glu(v_ref, g_ref):
        return v_ref[0] * jax.nn.sigmoid(g_ref[0])

    ext[0:halo, :] = jnp.where(i > 0, glu(vp_ref, gp_ref), 0.0)
    ext[halo:halo + ts, :] = glu(vc_ref, gc_ref)
    ext[halo + ts:, :] = jnp.where(i < last, glu(vn_ref, gn_ref), 0.0)
    rows = min(CONV_ROWS, ts)
    for cb in range(tc // LANES):
        cols = slice(cb * LANES, (cb + 1) * LANES)
        for r0 in range(0, ts, rows):
            acc = jnp.broadcast_to(b_ref[:, cols], (rows, LANES))
            for t in range(width):
                start = r0 + halo - pad + t
                acc = acc + w_ref[t:t + 1, cols] * ext[start:start + rows, cols]
            y_ref[0, r0:r0 + rows, cols] = acc


def _glu_conv(proj, conv_w, conv_b):
    b, s, _ = proj.shape
    width, cdim = conv_w.shape
    assert (width - 1) // 2 <= CONV_HALO
    ts = _pick(s, (256, 128))
    tc = _pick(cdim, (512, 256, 128))
    ncb, hb = cdim // tc, ts // CONV_HALO
    nhalo = s // CONV_HALO
    w_pad = jnp.pad(conv_w, ((0, -width % 8), (0, 0)))

    def cur(off):
        return pl.BlockSpec((1, ts, tc), lambda bi, i, c: (bi, i, off * ncb + c))

    def prev(off):
        return pl.BlockSpec((1, CONV_HALO, tc), lambda bi, i, c: (bi, jnp.maximum(i * hb - 1, 0), off * ncb + c))

    def nxt(off):
        return pl.BlockSpec((1, CONV_HALO, tc),
                            lambda bi, i, c: (bi, jnp.minimum((i + 1) * hb, nhalo - 1), off * ncb + c))

    return pl.pallas_call(
        functools.partial(_glu_conv_kernel, width=width),
        grid=(b, s // ts, ncb),
        in_specs=[pl.BlockSpec((w_pad.shape[0], tc), lambda bi, i, c: (0, c)),
                  pl.BlockSpec((1, tc), lambda bi, i, c: (0, c)),
                  cur(0), prev(0), nxt(0), cur(1), prev(1), nxt(1)],
        out_specs=pl.BlockSpec((1, ts, tc), lambda bi, i, c: (bi, i, c)),
        out_shape=jax.ShapeDtypeStruct((b, s, cdim), _F32),
        scratch_shapes=[pltpu.VMEM((ts + 2 * CONV_HALO, tc), _F32)],
        compiler_params=_params(("parallel", "parallel", "parallel"), 32),
        name="glu_conv",
    )(w_pad, conv_b.reshape(1, cdim), proj, proj, proj, proj, proj, proj)


SHORT_HALO = 8


def _norm_gate_kernel(y_ref, lg_ref, lb_ref, w_ref, db_ref, cc_ref, cp_ref, cn_ref, hc_ref, hp_ref, hn_ref,
                      o_ref, ext, *, width):
    i, last = pl.program_id(1), pl.num_programs(1) - 1
    ts, cdim = y_ref.shape[1], y_ref.shape[2]
    ddim = db_ref.shape[2]
    halo, pad = SHORT_HALO, (width - 1) // 2
    rows = min(CONV_ROWS, ts)

    for r0 in range(0, ts, rows):
        y = y_ref[0, r0:r0 + rows, :]
        yc = y - jnp.mean(y, axis=-1, keepdims=True)
        inv = lax.rsqrt(jnp.mean(yc * yc, axis=-1, keepdims=True) + NORM_EPS)
        n = yc * inv * lg_ref[...] + lb_ref[...]
        o_ref[0, r0:r0 + rows, 0:cdim] = (n * jax.nn.sigmoid(n)).astype(o_ref.dtype)

    ext[0:halo, :] = jnp.where(i > 0, cp_ref[0] * hp_ref[0], 0.0)
    ext[halo:halo + ts, :] = cc_ref[0] * hc_ref[0]
    ext[halo + ts:, :] = jnp.where(i < last, cn_ref[0] * hn_ref[0], 0.0)
    for cb in range(ddim // LANES):
        cols = slice(cb * LANES, (cb + 1) * LANES)
        for r0 in range(0, ts, rows):
            acc = jnp.zeros((rows, LANES), _F32)
            for t in range(width):
                start = r0 + halo - pad + t
                acc = acc + w_ref[t:t + 1, cols] * ext[start:start + rows, cols]
            o_ref[0, r0:r0 + rows, cdim + cb * LANES:cdim + (cb + 1) * LANES] = (
                db_ref[0, r0:r0 + rows, cols] * acc).astype(o_ref.dtype)


def _norm_gate(y, proj, ln_g, ln_b, conv_d_w):
    b, s, cdim = y.shape
    width, ddim = conv_d_w.shape
    assert (width - 1) // 2 <= SHORT_HALO and cdim % ddim == 0
    ts = _pick(s, (256, 128))
    hb, nhalo = ts // SHORT_HALO, s // SHORT_HALO
    base = 2 * cdim // ddim
    w_pad = jnp.pad(conv_d_w, ((0, -width % 8), (0, 0)))

    def cur(off):
        return pl.BlockSpec((1, ts, ddim), lambda bi, i: (bi, i, base + off))

    def prev(off):
        return pl.BlockSpec((1, SHORT_HALO, ddim), lambda bi, i: (bi, jnp.maximum(i * hb - 1, 0), base + off))

    def nxt(off):
        return pl.BlockSpec((1, SHORT_HALO, ddim),
                            lambda bi, i: (bi, jnp.minimum((i + 1) * hb, nhalo - 1), base + off))

    vec = pl.BlockSpec((1, cdim), lambda bi, i: (0, 0))
    return pl.pallas_call(
        functools.partial(_norm_gate_kernel, width=width),
        grid=(b, s // ts),
        in_specs=[pl.BlockSpec((1, ts, cdim), lambda bi, i: (bi, i, 0)), vec, vec,
                  pl.BlockSpec((w_pad.shape[0], ddim), lambda bi, i: (0, 0)),
                  cur(0), cur(1), prev(1), nxt(1), cur(2), prev(2), nxt(2)],
        out_specs=pl.BlockSpec((1, ts, cdim + ddim), lambda bi, i: (bi, i, 0)),
        out_shape=jax.ShapeDtypeStruct((b, s, cdim + ddim), _BF16),
        scratch_shapes=[pltpu.VMEM((ts + 2 * SHORT_HALO, ddim), _F32)],
        compiler_params=_params(("parallel", "parallel"), 48),
        name="norm_gate",
    )(y, ln_g.reshape(1, cdim), ln_b.reshape(1, cdim), w_pad, proj, proj, proj, proj, proj, proj, proj)


def _rmsnorm_kernel(x_ref, g_ref, o_ref):
    x = x_ref[...]
    inv = lax.rsqrt(jnp.mean(x * x, axis=-1, keepdims=True) + NORM_EPS)
    o_ref[...] = x * inv * g_ref[...]


def _rmsnorm(x, g):
    m, d = x.shape
    tm = _pick(m, (256, 128))
    return pl.pallas_call(
        _rmsnorm_kernel,
        grid=(m // tm,),
        in_specs=[pl.BlockSpec((tm, d), lambda i: (i, 0)), pl.BlockSpec((1, d), lambda i: (0, 0))],
        out_specs=pl.BlockSpec((tm, d), lambda i: (i, 0)),
        out_shape=jax.ShapeDtypeStruct((m, d), _F32),
        compiler_params=_params(("parallel",), 32),
        name="final_norm",
    )(x, g.reshape(1, d))


def _mixer_ab(x, g, w_in, w_out, rel_bias, b, s):
    qkv_dim = HEADS_A * HEAD_DIM
    w_in = w_in.astype(_BF16)
    qkv = _norm_matmul(x, g, (w_in[:, :3 * qkv_dim],), _BF16, "in_ab_qkv")
    u = _norm_matmul(x, g, (w_in[:, 3 * qkv_dim:],), _F32, "in_ab_u")
    a_out = _dilated_mixture(qkv.reshape(b, s, 3 * qkv_dim), rel_bias)
    b_out = _fourier_mix(u.reshape(b, s, -1))
    mixed = jnp.concatenate([a_out, b_out], axis=-1).reshape(b * s, -1)
    return _matmul_resid(mixed, w_out.astype(_BF16), x, 1.0, "out_ab")


def _mixer_cd(x, g, w_in, conv_c_w, conv_c_b, ln_c_g, ln_c_b, conv_d_w, w_out, b, s):
    proj = _norm_matmul(x, g, (w_in.astype(_BF16),), _F32, "in_cd").reshape(b, s, -1)
    y = _glu_conv(proj, conv_c_w, conv_c_b)
    mixed = _norm_gate(y, proj, ln_c_g, ln_c_b, conv_d_w).reshape(b * s, -1)
    return _matmul_resid(mixed, w_out.astype(_BF16), x, 1.0, "out_cd")


def kernel(x, ffn1_norm, ffn1_w_gate, ffn1_w_up, ffn1_w_down, mix_norm, ffn2_norm, ffn2_w_gate, ffn2_w_up,
           ffn2_w_down, rel_bias, w_in_ab, w_out_ab, w_in_cd, conv_c_w, conv_c_b, ln_c_g, ln_c_b, conv_d_w,
           w_out_cd, final_norm):
    b, s, d = x.shape
    x = x.reshape(b * s, d)
    for layer in range(ffn1_norm.shape[0]):
        x = _ffn(x, ffn1_norm[layer], ffn1_w_gate[layer], ffn1_w_up[layer], ffn1_w_down[layer], f"ffn1_l{layer}")
        i = layer // 2
        if layer % 2 == 0:
            x = _mixer_ab(x, mix_norm[layer], w_in_ab[i], w_out_ab[i], rel_bias, b, s)
        else:
            x = _mixer_cd(x, mix_norm[layer], w_in_cd[i], conv_c_w[i], conv_c_b[i], ln_c_g[i], ln_c_b[i],
                          conv_d_w[i], w_out_cd[i], b, s)
        x = _ffn(x, ffn2_norm[layer], ffn2_w_gate[layer], ffn2_w_up[layer], ffn2_w_down[layer], f"ffn2_l{layer}")
    return _rmsnorm(x, final_norm).reshape(b, s, d)
```

```python
import functools
import math

import jax
import jax.numpy as jnp
import numpy as np
from jax import lax
from jax.experimental import pallas as pl
from jax.experimental.pallas import tpu as pltpu

HEAD_DIM = 128
HEADS_A = 24
A_BRANCHES = ((128, 1), (512, 4), (2048, 16))
FNET_GROUPS = 8
FNET_GROUP_DIM = 128
REL_BUCKETS = 32
REL_MAX_DISTANCE = 1024
CONV_C_DIM = 2048
CONV_D_DIM = 2048
FFN_RESIDUAL = 0.5
NORM_EPS = 1e-6
NEG_INF = -1e30

LANES = 128
ATT_TQ = 128
ATT_HALO = 64
ATT_UNROLL = 4
ATT_OUT_ROWS = 256
ATT_CHUNK = 2048
MIB = 1024 * 1024

_F32 = jnp.float32
_BF16 = jnp.bfloat16


def _params(sem, vmem_mib):
    return pltpu.CompilerParams(dimension_semantics=sem, vmem_limit_bytes=vmem_mib * MIB)


def _pick(n, prefs):
    for p in prefs:
        if n % p == 0:
            return p
    return n


def _rms(x, g):
    inv = lax.rsqrt(jnp.mean(x * x, axis=-1, keepdims=True) + NORM_EPS)
    return x * inv * g


def _norm_matmul_kernel(x_ref, g_ref, w_ref, o_ref, h_ref):
    @pl.when(pl.program_id(1) == 0)
    def _():
        h_ref[...] = _rms(x_ref[...], g_ref[...]).astype(h_ref.dtype)

    o_ref[...] = jnp.dot(h_ref[...], w_ref[...], preferred_element_type=_F32).astype(o_ref.dtype)


def _norm_matmul(x, g, w, out_dtype, name):
    m, d = x.shape
    n = w.shape[1]
    tm = _pick(m, (512, 256, 128))
    tn = _pick(n, (512, 256, 128))
    return pl.pallas_call(
        _norm_matmul_kernel,
        grid=(m // tm, n // tn),
        in_specs=[pl.BlockSpec((tm, d), lambda i, j: (i, 0)),
                  pl.BlockSpec((1, d), lambda i, j: (0, 0)),
                  pl.BlockSpec((d, tn), lambda i, j: (0, j))],
        out_specs=pl.BlockSpec((tm, tn), lambda i, j: (i, j)),
        out_shape=jax.ShapeDtypeStruct((m, n), out_dtype),
        scratch_shapes=[pltpu.VMEM((tm, d), _BF16)],
        compiler_params=_params(("parallel", "arbitrary"), 48),
        name=name,
    )(x, g.reshape(1, d), w)


def _rmsnorm_kernel(x_ref, g_ref, o_ref):
    o_ref[...] = _rms(x_ref[...], g_ref[...]).astype(o_ref.dtype)


def _rmsnorm(x, g, out_dtype, name):
    m, d = x.shape
    tm = _pick(m, (256, 128))
    return pl.pallas_call(
        _rmsnorm_kernel,
        grid=(m // tm,),
        in_specs=[pl.BlockSpec((tm, d), lambda i: (i, 0)), pl.BlockSpec((1, d), lambda i: (0, 0))],
        out_specs=pl.BlockSpec((tm, d), lambda i: (i, 0)),
        out_shape=jax.ShapeDtypeStruct((m, d), out_dtype),
        compiler_params=_params(("parallel",), 32),
        name=name,
    )(x, g.reshape(1, d))


def _matmul_resid_kernel(*refs, n, scale):
    a_refs, w_refs, x_ref, o_ref = refs[:n], refs[n:2 * n], refs[2 * n], refs[2 * n + 1]
    acc = jnp.dot(a_refs[0][...], w_refs[0][...], preferred_element_type=_F32)
    for a_ref, w_ref in zip(a_refs[1:], w_refs[1:]):
        acc = acc + jnp.dot(a_ref[...], w_ref[...], preferred_element_type=_F32)
    o_ref[...] = x_ref[...] + (acc if scale == 1.0 else scale * acc)


def _matmul_resid(pieces, w, x, scale, name):
    m, n = x.shape
    ktot = w.shape[0]
    tm = _pick(m, (1024, 512, 256, 128)) if ktot <= 4096 else _pick(m, (512, 256, 128))
    tn = _pick(n, (512, 256, 128))
    a_specs, w_specs, row = [], [], 0
    for a in pieces:
        kp = a.shape[1]
        assert row % kp == 0
        a_specs.append(pl.BlockSpec((tm, kp), lambda i, j: (i, 0)))
        w_specs.append(pl.BlockSpec((kp, tn), lambda i, j, rb=row // kp: (rb, j)))
        row += kp
    assert row == ktot
    xo_spec = pl.BlockSpec((tm, tn), lambda i, j: (i, j))
    return pl.pallas_call(
        functools.partial(_matmul_resid_kernel, n=len(pieces), scale=scale),
        grid=(m // tm, n // tn),
        in_specs=a_specs + w_specs + [xo_spec],
        out_specs=xo_spec,
        out_shape=jax.ShapeDtypeStruct((m, n), _F32),
        compiler_params=_params(("parallel", "parallel"), 56),
        name=name,
    )(*pieces, *([w] * len(pieces)), x)


def _ffn_up_kernel(h_ref, wg_ref, wu_ref, wd_ref, o_ref, wdo_ref, wcat):
    tn = wg_ref.shape[2]

    @pl.when(pl.program_id(1) == 0)
    def _():
        wcat[:, :tn] = wg_ref[0].astype(wcat.dtype)
        wcat[:, tn:] = wu_ref[0].astype(wcat.dtype)
        wdo_ref[...] = wd_ref[0].astype(wdo_ref.dtype)

    r = jnp.dot(h_ref[...], wcat[...], preferred_element_type=_F32)
    a, b = r[:, :tn], r[:, tn:]
    o_ref[...] = (a * jax.nn.sigmoid(a) * b).astype(o_ref.dtype)


def _ffn_up(h, w_gate, w_up, w_down, layer, name):
    m, d = h.shape
    f = w_gate.shape[2]
    tm = _pick(m, (1024, 512, 256, 128))
    tn = _pick(f, (256, 128))
    w_spec = pl.BlockSpec((1, d, tn), lambda j, i: (layer, 0, j))
    return pl.pallas_call(
        _ffn_up_kernel,
        grid=(f // tn, m // tm),
        in_specs=[pl.BlockSpec((tm, d), lambda j, i: (i, 0)), w_spec, w_spec,
                  pl.BlockSpec((1, tn, d), lambda j, i: (layer, j, 0))],
        out_specs=[pl.BlockSpec((tm, tn), lambda j, i: (i, j)),
                   pl.BlockSpec((tn, d), lambda j, i: (j, 0))],
        out_shape=[jax.ShapeDtypeStruct((m, f), _BF16), jax.ShapeDtypeStruct((f, d), _BF16)],
        scratch_shapes=[pltpu.VMEM((d, 2 * tn), _BF16)],
        compiler_params=_params(("arbitrary", "arbitrary"), 56),
        name=name,
    )(h, w_gate, w_up, w_down)


def _ffn(x, g, w_gate, w_up, w_down, layer, name):
    h = _rmsnorm(x, g[layer], _BF16, name + "_norm")
    hidden, wd = _ffn_up(h, w_gate, w_up, w_down, layer, name + "_up")
    return _matmul_resid([hidden], wd, x, FFN_RESIDUAL, name + "_down")


def _t5_bucket_np(rel):
    half = REL_BUCKETS // 2
    max_exact = half // 2
    ret = np.where(rel > 0, half, 0)
    n = np.abs(rel)
    nf = np.maximum(n, 1).astype(np.float64)
    val = np.log(nf / max_exact) / math.log(REL_MAX_DISTANCE / max_exact) * (half - max_exact)
    large = np.minimum(max_exact + val.astype(np.int64), half - 1)
    return (ret + np.where(n < max_exact, n, large)).astype(np.int32)


def _attn_bias_tiles(rel_bias):
    tq, halo = ATT_TQ, ATT_HALO
    win = tq + 2 * halo
    period = 2 * win
    c = np.arange(win)[None, :]
    ok_first = np.broadcast_to(c >= halo, (tq, win))
    ok_last = np.broadcast_to(c < tq + halo, (tq, win))
    branches = []
    for _, dilation in A_BRANCHES:
        rel = np.arange(-halo, halo + 1)
        band = rel_bias[_t5_bucket_np(rel * dilation)].astype(_F32).T
        heads = band.shape[0]
        row = jnp.full((heads, period), NEG_INF, _F32).at[:, :2 * halo + 1].set(band)
        toeplitz = jnp.tile(row, (1, tq))[:, :tq * (period - 1)].reshape(heads, tq, period - 1)[:, :, :win]
        branches.append(jnp.stack(
            [jnp.where(mk[None], toeplitz, NEG_INF)
             for mk in (np.ones((tq, win), bool), ok_first, ok_last, ok_first & ok_last)], axis=1))
    return jnp.stack(branches, axis=1)


def _attn_kernel(bm_ref, q_ref, k_ref, v_ref, o_ref, wide, qd, kd, vd, acc, mrun, lrun, *, scale, dilations):
    seq = q_ref.shape[1]
    tq, halo = ATT_TQ, ATT_HALO
    win = tq + 2 * halo
    ntiles = seq // tq
    zeros = jnp.zeros((halo, HEAD_DIM), kd.dtype)

    for g, d in enumerate(dilations):
        sub = seq // d
        nt = sub // tq
        chunk = wide.shape[0]
        per = chunk // d
        for src, dst, padded in ((q_ref, qd, False), (k_ref, kd, True), (v_ref, vd, True)):
            if d == 1 and not padded:
                continue
            stride_r = sub + 2 * halo if padded else sub
            first = halo if padded else 0
            for c in range(seq // chunk):
                def move(i, carry, c=c, src=src, dst=dst, first=first, widen=d > 1):
                    off = pl.multiple_of(i * ATT_OUT_ROWS, ATT_OUT_ROWS)
                    rows = src[0, pl.ds(c * chunk + off, ATT_OUT_ROWS), :]
                    if widen:
                        wide[pl.ds(off, ATT_OUT_ROWS), :] = rows.astype(_F32)
                    else:
                        dst[pl.ds(pl.multiple_of(first + c * chunk + off, ATT_HALO), ATT_OUT_ROWS), :] = rows
                    return carry

                lax.fori_loop(0, chunk // ATT_OUT_ROWS, move, 0)
                if d == 1:
                    continue
                for r in range(d):
                    base = r * stride_r + first + c * per
                    dst[base:base + per, :] = wide[pl.ds(r, per, stride=d), :].astype(dst.dtype)
            if padded:
                for r in range(d):
                    dst[r * stride_r:r * stride_r + halo, :] = zeros
                    dst[r * stride_r + halo + sub:(r + 1) * stride_r, :] = zeros

        def tile(tile_idx, g=g, d=d, nt=nt):
            r = tile_idx // nt
            t = tile_idx - r * nt
            q0 = pl.multiple_of(tile_idx * tq, tq)
            k0 = pl.multiple_of(tile_idx * tq + r * (2 * halo), 2 * halo)
            q = q_ref[0, pl.ds(q0, tq), :] if d == 1 else qd[pl.ds(q0, tq), :]
            kw = kd[pl.ds(k0, win), :]
            vw = vd[pl.ds(k0, win), :]
            s = lax.dot_general(q, kw, (((1,), (1,)), ((), ())), preferred_element_type=_F32)
            variant = (t == 0).astype(jnp.int32) + 2 * (t == nt - 1).astype(jnp.int32)
            s = s * scale + bm_ref[0, g, variant]
            mx = jnp.max(s, axis=-1, keepdims=True)
            p = jnp.exp(s - mx)
            den = jnp.sum(p, axis=-1, keepdims=True)
            pv = jnp.dot(p.astype(vw.dtype), vw, preferred_element_type=_F32)
            idx = pl.ds(q0, tq) if d == 1 else pl.ds(t * (tq * d) + r, tq, stride=d)
            if g == 0:
                acc[idx, :] = pv
                mrun[idx, :] = jnp.broadcast_to(mx, (tq, LANES))
                lrun[idx, :] = jnp.broadcast_to(den, (tq, LANES))
            else:
                m_old = mrun[idx, :]
                m_new = jnp.maximum(m_old, mx)
                c_old = jnp.exp(m_old - m_new)
                c_new = jnp.exp(mx - m_new)
                acc[idx, :] = acc[idx, :] * c_old + pv * c_new
                lrun[idx, :] = lrun[idx, :] * c_old + den * c_new
                mrun[idx, :] = m_new

        def tiles(it, carry, tile=tile):
            for u in range(ATT_UNROLL):
                tile(it * ATT_UNROLL + u)
            return carry

        lax.fori_loop(0, ntiles // ATT_UNROLL, tiles, 0)

    def finish(it, carry):
        rows = pl.ds(pl.multiple_of(it * ATT_OUT_ROWS, ATT_OUT_ROWS), ATT_OUT_ROWS)
        o_ref[0, rows, :] = (acc[rows, :] / lrun[rows, :]).astype(o_ref.dtype)
        return carry

    lax.fori_loop(0, seq // ATT_OUT_ROWS, finish, 0)


def _dilated_mixture(qkv, rel_bias):
    b, s, _ = qkv.shape
    h, e = HEADS_A, HEAD_DIM
    dilations = tuple(d for _, d in A_BRANCHES)
    for window, d in A_BRANCHES:
        assert window // (2 * d) == ATT_HALO and (s // d) % ATT_TQ == 0
    assert (s // ATT_TQ) % ATT_UNROLL == 0 and s % ATT_OUT_ROWS == 0
    chunk = min(ATT_CHUNK, s)
    assert s % chunk == 0 and chunk % ATT_OUT_ROWS == 0 and all((chunk // d) % 16 == 0 for d in dilations)
    win = ATT_TQ + 2 * ATT_HALO
    pad_rows = s + 2 * ATT_HALO * max(dilations)

    def in_spec(which):
        return pl.BlockSpec((1, s, e), lambda hi, bi: (bi, 0, which * h + hi))

    return pl.pallas_call(
        functools.partial(_attn_kernel, scale=1.0 / math.sqrt(e), dilations=dilations),
        grid=(h, b),
        in_specs=[pl.BlockSpec((1, len(dilations), 4, ATT_TQ, win), lambda hi, bi: (hi, 0, 0, 0, 0)),
                  in_spec(0), in_spec(1), in_spec(2)],
        out_specs=pl.BlockSpec((1, s, e), lambda hi, bi: (bi, 0, hi)),
        out_shape=jax.ShapeDtypeStruct((b, s, h * e), _BF16),
        scratch_shapes=[pltpu.VMEM((chunk, e), _F32), pltpu.VMEM((s, e), _BF16),
                        pltpu.VMEM((pad_rows, e), _BF16), pltpu.VMEM((pad_rows, e), _BF16),
                        pltpu.VMEM((s, e), _F32), pltpu.VMEM((s, LANES), _F32), pltpu.VMEM((s, LANES), _F32)],
        compiler_params=_params(("parallel", "parallel"), 56),
        name="dilated_attention",
    )(_attn_bias_tiles(rel_bias), qkv, qkv, qkv)


_HI = lax.Precision.HIGHEST


def _fft_stage1_kernel(f_ref, u_ref, ar_ref, ai_ref):
    n1 = u_ref.shape[1]
    r = jnp.dot(f_ref[...], u_ref[0], precision=_HI, preferred_element_type=_F32)
    ar_ref[0] = r[:n1]
    ai_ref[0] = r[n1:]


def _fft_stage2_kernel(twc_ref, tws_ref, c2_ref, s2_ref, cc_ref, sc_ref, ar_ref, ai_ref, o_ref):
    dot = functools.partial(jnp.dot, precision=_HI, preferred_element_type=_F32)
    ar, ai = ar_ref[0], ai_ref[0]
    c, s = twc_ref[...], tws_ref[...]
    pr = ar * c + ai * s
    pi = ai * c - ar * s
    c2, s2 = c2_ref[...], s2_ref[...]
    zr = dot(c2, pr) + dot(s2, pi)
    zi = dot(c2, pi) - dot(s2, pr)
    gd = cc_ref.shape[0]
    for g in range(ar.shape[1] // gd):
        cols = slice(g * gd, (g + 1) * gd)
        f = dot(zr[:, cols], cc_ref[...]) + dot(zi[:, cols], sc_ref[...])
        o_ref[0, :, cols] = f.astype(o_ref.dtype)


def _fourier_mix(u):
    b, s, width = u.shape
    gd = FNET_GROUP_DIM
    n2 = 128
    n1 = s // n2
    assert n1 * n2 == s and n1 % 8 == 0

    def trig(n, rows, cols):
        ang = 2.0 * np.pi * ((np.arange(rows)[:, None] * np.arange(cols)[None, :]) % n) / n
        return np.cos(ang), np.sin(ang)

    c1, s1 = trig(n1, n1, n1)
    f1 = jnp.asarray(np.concatenate([c1, -s1], axis=0), _F32)
    tc, tsn = trig(s, n1, n2)
    twc = jnp.asarray(tc.reshape(s, 1), _F32)
    tws = jnp.asarray(tsn.reshape(s, 1), _F32)
    c2, s2 = trig(n2, n2, n2)
    norm = 1.0 / math.sqrt(s * gd)
    cc, sc = trig(gd, gd, gd)
    c2, s2 = jnp.asarray(c2, _F32), jnp.asarray(s2, _F32)
    cc, sc = jnp.asarray(cc * norm, _F32), jnp.asarray(sc * norm, _F32)

    lanes = n2 * width
    tl = _pick(lanes, (4096, 2048, 1024, 512, 256, 128))
    a_spec = pl.BlockSpec((1, n1, tl), lambda bi, j: (bi, 0, j))
    ar, ai = pl.pallas_call(
        _fft_stage1_kernel,
        grid=(b, lanes // tl),
        in_specs=[pl.BlockSpec((2 * n1, n1), lambda bi, j: (0, 0)), a_spec],
        out_specs=[a_spec, a_spec],
        out_shape=[jax.ShapeDtypeStruct((b, n1, lanes), _F32)] * 2,
        compiler_params=_params(("parallel", "parallel"), 32),
        name="fft_stage1",
    )(f1, u.reshape(b, n1, lanes))

    tw_spec = pl.BlockSpec((n2, 1), lambda bi, k1: (k1, 0))
    m_spec = pl.BlockSpec((n2, n2), lambda bi, k1: (0, 0))
    g_spec = pl.BlockSpec((gd, gd), lambda bi, k1: (0, 0))
    z_spec = pl.BlockSpec((1, n2, width), lambda bi, k1: (bi, k1, 0))
    out = pl.pallas_call(
        _fft_stage2_kernel,
        grid=(b, n1),
        in_specs=[tw_spec, tw_spec, m_spec, m_spec, g_spec, g_spec, z_spec, z_spec],
        out_specs=pl.BlockSpec((1, n2, width), lambda bi, k1: (bi, 0, k1)),
        out_shape=jax.ShapeDtypeStruct((b, n2, n1 * width), _BF16),
        compiler_params=_params(("parallel", "parallel"), 32),
        name="fft_stage2",
    )(twc, tws, c2, s2, cc, sc, ar.reshape(b, s, width), ai.reshape(b, s, width))
    return out.reshape(b, s, width)


CONV_HALO = 16
CONV_ROWS = 64


def _glu_conv_kernel(w_ref, b_ref, vc_ref, vp_ref, vn_ref, gc_ref, gp_ref, gn_ref, y_ref, ext, *, width):
    i, last = pl.program_id(1), pl.num_programs(1) - 1
    ts, tc = vc_ref.shape[1], vc_ref.shape[2]
    halo, pad = CONV_HALO, (width - 1) // 2

    def glu(v_ref, g_ref):
        return v_ref[0] * jax.nn.sigmoid(g_ref[0])

    ext[0:halo, :] = jnp.where(i > 0, glu(vp_ref, gp_ref), 0.0)
    ext[halo:halo + ts, :] = glu(vc_ref, gc_ref)
    ext[halo + ts:, :] = jnp.where(i < last, glu(vn_ref, gn_ref), 0.0)
    rows = min(CONV_ROWS, ts)
    for cb in range(tc // LANES):
        cols = slice(cb * LANES, (cb + 1) * LANES)
        for r0 in range(0, ts, rows):
            acc = jnp.broadcast_to(b_ref[:, cols], (rows, LANES))
            for t in range(width):
                start = r0 + halo - pad + t
                acc = acc + w_ref[t:t + 1, cols] * ext[start:start + rows, cols]
            y_ref[0, r0:r0 + rows, cols] = acc


def _glu_conv(proj, conv_w, conv_b):
    b, s, _ = proj.shape
    width, cdim = conv_w.shape
    assert (width - 1) // 2 <= CONV_HALO
    ts = _pick(s, (256, 128))
    tc = _pick(cdim, (512, 256, 128))
    ncb, hb = cdim // tc, ts // CONV_HALO
    nhalo = s // CONV_HALO
    w_pad = jnp.pad(conv_w, ((0, -width % 8), (0, 0)))

    def cur(off):
        return pl.BlockSpec((1, ts, tc), lambda bi, i, c: (bi, i, off * ncb + c))

    def prev(off):
        return pl.BlockSpec((1, CONV_HALO, tc), lambda bi, i, c: (bi, jnp.maximum(i * hb - 1, 0), off * ncb + c))

    def nxt(off):
        return pl.BlockSpec((1, CONV_HALO, tc),
                            lambda bi, i, c: (bi, jnp.minimum((i + 1) * hb, nhalo - 1), off * ncb + c))

    return pl.pallas_call(
        functools.partial(_glu_conv_kernel, width=width),
        grid=(b, s // ts, ncb),
        in_specs=[pl.BlockSpec((w_pad.shape[0], tc), lambda bi, i, c: (0, c)),
                  pl.BlockSpec((1, tc), lambda bi, i, c: (0, c)),
                  cur(0), prev(0), nxt(0), cur(1), prev(1), nxt(1)],
        out_specs=pl.BlockSpec((1, ts, tc), lambda bi, i, c: (bi, i, c)),
        out_shape=jax.ShapeDtypeStruct((b, s, cdim), _F32),
        scratch_shapes=[pltpu.VMEM((ts + 2 * CONV_HALO, tc), _F32)],
        compiler_params=_params(("parallel", "parallel", "parallel"), 32),
        name="glu_conv",
    )(w_pad, conv_b.reshape(1, cdim), proj, proj, proj, proj, proj, proj)


SHORT_HALO = 8


def _norm_gate_kernel(y_ref, lg_ref, lb_ref, w_ref, db_ref, cc_ref, cp_ref, cn_ref, hc_ref, hp_ref, hn_ref,
                      o_ref, ext, *, width):
    i, last = pl.program_id(1), pl.num_programs(1) - 1
    ts, cdim = y_ref.shape[1], y_ref.shape[2]
    ddim = db_ref.shape[2]
    halo, pad = SHORT_HALO, (width - 1) // 2
    rows = min(CONV_ROWS, ts)

    for r0 in range(0, ts, rows):
        y = y_ref[0, r0:r0 + rows, :]
        yc = y - jnp.mean(y, axis=-1, keepdims=True)
        inv = lax.rsqrt(jnp.mean(yc * yc, axis=-1, keepdims=True) + NORM_EPS)
        n = yc * inv * lg_ref[...] + lb_ref[...]
        o_ref[0, r0:r0 + rows, 0:cdim] = (n * jax.nn.sigmoid(n)).astype(o_ref.dtype)

    ext[0:halo, :] = jnp.where(i > 0, cp_ref[0] * hp_ref[0], 0.0)
    ext[halo:halo + ts, :] = cc_ref[0] * hc_ref[0]
    ext[halo + ts:, :] = jnp.where(i < last, cn_ref[0] * hn_ref[0], 0.0)
    for cb in range(ddim // LANES):
        cols = slice(cb * LANES, (cb + 1) * LANES)
        for r0 in range(0, ts, rows):
            acc = jnp.zeros((rows, LANES), _F32)
            for t in range(width):
                start = r0 + halo - pad + t
                acc = acc + w_ref[t:t + 1, cols] * ext[start:start + rows, cols]
            o_ref[0, r0:r0 + rows, cdim + cb * LANES:cdim + (cb + 1) * LANES] = (
                db_ref[0, r0:r0 + rows, cols] * acc).astype(o_ref.dtype)


def _norm_gate(y, proj, ln_g, ln_b, conv_d_w):
    b, s, cdim = y.shape
    width, ddim = conv_d_w.shape
    assert (width - 1) // 2 <= SHORT_HALO and cdim % ddim == 0
    ts = _pick(s, (256, 128))
    hb, nhalo = ts // SHORT_HALO, s // SHORT_HALO
    base = 2 * cdim // ddim
    w_pad = jnp.pad(conv_d_w, ((0, -width % 8), (0, 0)))

    def cur(off):
        return pl.BlockSpec((1, ts, ddim), lambda bi, i: (bi, i, base + off))

    def prev(off):
        return pl.BlockSpec((1, SHORT_HALO, ddim), lambda bi, i: (bi, jnp.maximum(i * hb - 1, 0), base + off))

    def nxt(off):
        return pl.BlockSpec((1, SHORT_HALO, ddim),
                            lambda bi, i: (bi, jnp.minimum((i + 1) * hb, nhalo - 1), base + off))

    vec = pl.BlockSpec((1, cdim), lambda bi, i: (0, 0))
    return pl.pallas_call(
        functools.partial(_norm_gate_kernel, width=width),
        grid=(b, s // ts),
        in_specs=[pl.BlockSpec((1, ts, cdim), lambda bi, i: (bi, i, 0)), vec, vec,
                  pl.BlockSpec((w_pad.shape[0], ddim), lambda bi, i: (0, 0)),
                  cur(0), cur(1), prev(1), nxt(1), cur(2), prev(2), nxt(2)],
        out_specs=pl.BlockSpec((1, ts, cdim + ddim), lambda bi, i: (bi, i, 0)),
        out_shape=jax.ShapeDtypeStruct((b, s, cdim + ddim), _BF16),
        scratch_shapes=[pltpu.VMEM((ts + 2 * SHORT_HALO, ddim), _F32)],
        compiler_params=_params(("parallel", "parallel"), 48),
        name="norm_gate",
    )(y, ln_g.reshape(1, cdim), ln_b.reshape(1, cdim), w_pad, proj, proj, proj, proj, proj, proj, proj)


def _mixer_ab(x, g, w_in, w_out, rel_bias, b, s):
    qkv_dim = HEADS_A * HEAD_DIM
    w_in = w_in.astype(_BF16)
    qkv = _norm_matmul(x, g, w_in[:, :3 * qkv_dim], _BF16, "in_ab_qkv")
    u = _norm_matmul(x, g, w_in[:, 3 * qkv_dim:], _F32, "in_ab_u")
    a_out = _dilated_mixture(qkv.reshape(b, s, 3 * qkv_dim), rel_bias).reshape(b * s, qkv_dim)
    b_out = _fourier_mix(u.reshape(b, s, -1)).reshape(b * s, -1)
    return _matmul_resid([a_out, b_out], w_out.astype(_BF16), x, 1.0, "out_ab")


def _mixer_cd(x, g, w_in, conv_c_w, conv_c_b, ln_c_g, ln_c_b, conv_d_w, w_out, b, s):
    proj = _norm_matmul(x, g, w_in.astype(_BF16), _F32, "in_cd").reshape(b, s, -1)
    y = _glu_conv(proj, conv_c_w, conv_c_b)
    mixed = _norm_gate(y, proj, ln_c_g, ln_c_b, conv_d_w).reshape(b * s, -1)
    return _matmul_resid([mixed], w_out.astype(_BF16), x, 1.0, "out_cd")


def kernel(x, ffn1_norm, ffn1_w_gate, ffn1_w_up, ffn1_w_down, mix_norm, ffn2_norm, ffn2_w_gate, ffn2_w_up,
           ffn2_w_down, rel_bias, w_in_ab, w_out_ab, w_in_cd, conv_c_w, conv_c_b, ln_c_g, ln_c_b, conv_d_w,
           w_out_cd, final_norm):
    b, s, d = x.shape
    x = x.reshape(b * s, d)
    for layer in range(ffn1_norm.shape[0]):
        x = _ffn(x, ffn1_norm, ffn1_w_gate, ffn1_w_up, ffn1_w_down, layer, f"ffn1_l{layer}")
        i = layer // 2
        if layer % 2 == 0:
            x = _mixer_ab(x, mix_norm[layer], w_in_ab[i], w_out_ab[i], rel_bias, b, s)
        else:
            x = _mixer_cd(x, mix_norm[layer], w_in_cd[i], conv_c_w[i], conv_c_b[i], ln_c_g[i], ln_c_b[i],
                          conv_d_w[i], w_out_cd[i], b, s)
        x = _ffn(x, ffn2_norm, ffn2_w_gate, ffn2_w_up, ffn2_w_down, layer, f"ffn2_l{layer}")
    return _rmsnorm(x, final_norm, _F32, "final_norm").reshape(b, s, d)
```

```python
import functools
import math

import jax
import jax.numpy as jnp
import numpy as np
from jax import lax
from jax.experimental import pallas as pl
from jax.experimental.pallas import tpu as pltpu

HEAD_DIM = 128
HEADS_A = 24
A_BRANCHES = ((128, 1), (512, 4), (2048, 16))
FNET_GROUPS = 8
FNET_GROUP_DIM = 128
REL_BUCKETS = 32
REL_MAX_DISTANCE = 1024
CONV_C_DIM = 2048
CONV_D_DIM = 2048
FFN_RESIDUAL = 0.5
NORM_EPS = 1e-6
NEG_INF = -1e30

LANES = 128
ATT_TQ = 128
ATT_HALO = 64
ATT_UNROLL = 16
ATT_OUT_ROWS = 256
ATT_CHUNK = 2048
MIB = 1024 * 1024

_F32 = jnp.float32
_BF16 = jnp.bfloat16


def _params(sem, vmem_mib):
    return pltpu.CompilerParams(dimension_semantics=sem, vmem_limit_bytes=vmem_mib * MIB)


def _pick(n, prefs):
    for p in prefs:
        if n % p == 0:
            return p
    return n


def _rms(x, g):
    inv = lax.rsqrt(jnp.mean(x * x, axis=-1, keepdims=True) + NORM_EPS)
    return x * inv * g


def _norm_matmul_kernel(x_ref, g_ref, w_ref, o_ref, h_ref):
    @pl.when(pl.program_id(1) == 0)
    def _():
        h_ref[...] = _rms(x_ref[...], g_ref[...]).astype(h_ref.dtype)

    o_ref[...] = jnp.dot(h_ref[...], w_ref[...], preferred_element_type=_F32).astype(o_ref.dtype)


def _norm_matmul(x, g, w, out_dtype, name):
    m, d = x.shape
    n = w.shape[1]
    tm = _pick(m, (512, 256, 128))
    tn = _pick(n, (1024, 512, 256, 128))
    return pl.pallas_call(
        _norm_matmul_kernel,
        grid=(m // tm, n // tn),
        in_specs=[pl.BlockSpec((tm, d), lambda i, j: (i, 0)),
                  pl.BlockSpec((1, d), lambda i, j: (0, 0)),
                  pl.BlockSpec((d, tn), lambda i, j: (0, j))],
        out_specs=pl.BlockSpec((tm, tn), lambda i, j: (i, j)),
        out_shape=jax.ShapeDtypeStruct((m, n), out_dtype),
        scratch_shapes=[pltpu.VMEM((tm, d), _BF16)],
        compiler_params=_params(("parallel", "arbitrary"), 56),
        name=name,
    )(x, g.reshape(1, d), w)


def _rmsnorm_kernel(x_ref, g_ref, o_ref):
    o_ref[...] = _rms(x_ref[...], g_ref[...]).astype(o_ref.dtype)


def _rmsnorm(x, g, out_dtype, name):
    m, d = x.shape
    tm = _pick(m, (256, 128))
    return pl.pallas_call(
        _rmsnorm_kernel,
        grid=(m // tm,),
        in_specs=[pl.BlockSpec((tm, d), lambda i: (i, 0)), pl.BlockSpec((1, d), lambda i: (0, 0))],
        out_specs=pl.BlockSpec((tm, d), lambda i: (i, 0)),
        out_shape=jax.ShapeDtypeStruct((m, d), out_dtype),
        compiler_params=_params(("parallel",), 32),
        name=name,
    )(x, g.reshape(1, d))


def _matmul_resid_kernel(*refs, n, scale):
    a_refs, w_refs, x_ref, o_ref = refs[:n], refs[n:2 * n], refs[2 * n], refs[2 * n + 1]
    acc = jnp.dot(a_refs[0][...], w_refs[0][...], preferred_element_type=_F32)
    for a_ref, w_ref in zip(a_refs[1:], w_refs[1:]):
        acc = acc + jnp.dot(a_ref[...], w_ref[...], preferred_element_type=_F32)
    o_ref[...] = x_ref[...] + (acc if scale == 1.0 else scale * acc)


def _matmul_resid(pieces, w, x, scale, name):
    m, n = x.shape
    ktot = w.shape[0]
    tm = _pick(m, (1024, 512, 256, 128)) if ktot <= 4096 else _pick(m, (512, 256, 128))
    tn = _pick(n, (512, 256, 128))
    a_specs, w_specs, row = [], [], 0
    for a in pieces:
        kp = a.shape[1]
        assert row % kp == 0
        a_specs.append(pl.BlockSpec((tm, kp), lambda i, j: (i, 0)))
        w_specs.append(pl.BlockSpec((kp, tn), lambda i, j, rb=row // kp: (rb, j)))
        row += kp
    assert row == ktot
    xo_spec = pl.BlockSpec((tm, tn), lambda i, j: (i, j))
    return pl.pallas_call(
        functools.partial(_matmul_resid_kernel, n=len(pieces), scale=scale),
        grid=(m // tm, n // tn),
        in_specs=a_specs + w_specs + [xo_spec],
        out_specs=xo_spec,
        out_shape=jax.ShapeDtypeStruct((m, n), _F32),
        compiler_params=_params(("parallel", "parallel"), 56),
        name=name,
    )(*pieces, *([w] * len(pieces)), x)


def _ffn_up_kernel(h_ref, wg_ref, wu_ref, wd_ref, o_ref, wdo_ref, wcat):
    tn = wg_ref.shape[2]

    @pl.when(pl.program_id(1) == 0)
    def _():
        wcat[:, :tn] = wg_ref[0].astype(wcat.dtype)
        wcat[:, tn:] = wu_ref[0].astype(wcat.dtype)
        wdo_ref[...] = wd_ref[0].astype(wdo_ref.dtype)

    r = jnp.dot(h_ref[...], wcat[...], preferred_element_type=_F32)
    a, b = r[:, :tn], r[:, tn:]
    o_ref[...] = (a * jax.nn.sigmoid(a) * b).astype(o_ref.dtype)


def _ffn_up(h, w_gate, w_up, w_down, layer, name):
    m, d = h.shape
    f = w_gate.shape[2]
    tm = _pick(m, (1024, 512, 256, 128))
    tn = _pick(f, (256, 128))
    w_spec = pl.BlockSpec((1, d, tn), lambda j, i: (layer, 0, j))
    return pl.pallas_call(
        _ffn_up_kernel,
        grid=(f // tn, m // tm),
        in_specs=[pl.BlockSpec((tm, d), lambda j, i: (i, 0)), w_spec, w_spec,
                  pl.BlockSpec((1, tn, d), lambda j, i: (layer, j, 0))],
        out_specs=[pl.BlockSpec((tm, tn), lambda j, i: (i, j)),
                   pl.BlockSpec((tn, d), lambda j, i: (j, 0))],
        out_shape=[jax.ShapeDtypeStruct((m, f), _BF16), jax.ShapeDtypeStruct((f, d), _BF16)],
        scratch_shapes=[pltpu.VMEM((d, 2 * tn), _BF16)],
        compiler_params=_params(("arbitrary", "arbitrary"), 56),
        name=name,
    )(h, w_gate, w_up, w_down)


def _ffn(x, g, w_gate, w_up, w_down, layer, name):
    h = _rmsnorm(x, g[layer], _BF16, name + "_norm")
    hidden, wd = _ffn_up(h, w_gate, w_up, w_down, layer, name + "_up")
    return _matmul_resid([hidden], wd, x, FFN_RESIDUAL, name + "_down")


def _t5_bucket_np(rel):
    half = REL_BUCKETS // 2
    max_exact = half // 2
    ret = np.where(rel > 0, half, 0)
    n = np.abs(rel)
    nf = np.maximum(n, 1).astype(np.float64)
    val = np.log(nf / max_exact) / math.log(REL_MAX_DISTANCE / max_exact) * (half - max_exact)
    large = np.minimum(max_exact + val.astype(np.int64), half - 1)
    return (ret + np.where(n < max_exact, n, large)).astype(np.int32)


def _attn_bias_tiles(rel_bias):
    tq, halo = ATT_TQ, ATT_HALO
    win = tq + 2 * halo
    period = 2 * win
    c = np.arange(win)[None, :]
    ok_first = np.broadcast_to(c >= halo, (tq, win))
    ok_last = np.broadcast_to(c < tq + halo, (tq, win))
    branches = []
    for _, dilation in A_BRANCHES:
        rel = np.arange(-halo, halo + 1)
        band = rel_bias[_t5_bucket_np(rel * dilation)].astype(_F32).T
        heads = band.shape[0]
        row = jnp.full((heads, period), NEG_INF, _F32).at[:, :2 * halo + 1].set(band)
        toeplitz = jnp.tile(row, (1, tq))[:, :tq * (period - 1)].reshape(heads, tq, period - 1)[:, :, :win]
        branches.append(jnp.stack(
            [jnp.where(mk[None], toeplitz, NEG_INF)
             for mk in (np.ones((tq, win), bool), ok_first, ok_last, ok_first & ok_last)], axis=1))
    return jnp.stack(branches, axis=1)


def _attn_kernel(bm_ref, q_ref, k_ref, v_ref, o_ref, wide, qd, kd, vd, acc, mrun, lrun, *, scale, dilations):
    seq = q_ref.shape[1]
    tq, halo = ATT_TQ, ATT_HALO
    win = tq + 2 * halo
    ntiles = seq // tq
    zeros = jnp.zeros((halo, HEAD_DIM), kd.dtype)
    ones = jnp.ones((win, LANES), vd.dtype)

    for g, d in enumerate(dilations):
        sub = seq // d
        nt = sub // tq
        chunk = wide.shape[0]
        per = chunk // d
        for src, dst, padded in ((q_ref, qd, False), (k_ref, kd, True), (v_ref, vd, True)):
            if d == 1 and not padded:
                continue
            stride_r = sub + 2 * halo if padded else sub
            first = halo if padded else 0
            for c in range(seq // chunk):
                def move(i, carry, c=c, src=src, dst=dst, first=first, widen=d > 1):
                    off = pl.multiple_of(i * ATT_OUT_ROWS, ATT_OUT_ROWS)
                    rows = src[0, pl.ds(c * chunk + off, ATT_OUT_ROWS), :]
                    if widen:
                        wide[pl.ds(off, ATT_OUT_ROWS), :] = rows.astype(_F32)
                    else:
                        dst[pl.ds(pl.multiple_of(first + c * chunk + off, ATT_HALO), ATT_OUT_ROWS), :] = rows
                    return carry

                lax.fori_loop(0, chunk // ATT_OUT_ROWS, move, 0)
                if d == 1:
                    continue
                for r in range(d):
                    base = r * stride_r + first + c * per
                    dst[base:base + per, :] = wide[pl.ds(r, per, stride=d), :].astype(dst.dtype)
            if padded:
                for r in range(d):
                    dst[r * stride_r:r * stride_r + halo, :] = zeros
                    dst[r * stride_r + halo + sub:(r + 1) * stride_r, :] = zeros

        def tile(tile_idx, g=g, d=d, nt=nt):
            r = tile_idx // nt
            t = tile_idx - r * nt
            q0 = pl.multiple_of(tile_idx * tq, tq)
            k0 = pl.multiple_of(tile_idx * tq + r * (2 * halo), 2 * halo)
            q = q_ref[0, pl.ds(q0, tq), :] if d == 1 else qd[pl.ds(q0, tq), :]
            kw = kd[pl.ds(k0, win), :]
            vw = vd[pl.ds(k0, win), :]
            s = lax.dot_general(q, kw, (((1,), (1,)), ((), ())), preferred_element_type=_F32)
            variant = (t == 0).astype(jnp.int32) + 2 * (t == nt - 1).astype(jnp.int32)
            s = s * scale + bm_ref[0, g, variant]
            mx = jnp.max(s, axis=-1, keepdims=True)
            p = jnp.exp(s - mx)
            pv = jnp.dot(p.astype(vw.dtype), jnp.concatenate([vw, ones], axis=1), preferred_element_type=_F32)
            pv, den = pv[:, :HEAD_DIM], pv[:, HEAD_DIM:]
            idx = pl.ds(q0, tq) if d == 1 else pl.ds(t * (tq * d) + r, tq, stride=d)
            if g == 0:
                acc[idx, :] = pv
                mrun[idx, :] = jnp.broadcast_to(mx, (tq, LANES))
                lrun[idx, :] = den
            else:
                m_old = mrun[idx, :]
                m_new = jnp.maximum(m_old, mx)
                c_old = jnp.exp(m_old - m_new)
                c_new = jnp.exp(mx - m_new)
                acc[idx, :] = acc[idx, :] * c_old + pv * c_new
                lrun[idx, :] = lrun[idx, :] * c_old + den * c_new
                mrun[idx, :] = m_new

        def tiles(it, carry, tile=tile):
            for u in range(ATT_UNROLL):
                tile(it * ATT_UNROLL + u)
            return carry

        lax.fori_loop(0, ntiles // ATT_UNROLL, tiles, 0)

    def finish(it, carry):
        rows = pl.ds(pl.multiple_of(it * ATT_OUT_ROWS, ATT_OUT_ROWS), ATT_OUT_ROWS)
        o_ref[0, rows, :] = (acc[rows, :] / lrun[rows, :]).astype(o_ref.dtype)
        return carry

    lax.fori_loop(0, seq // ATT_OUT_ROWS, finish, 0)


def _dilated_mixture(qkv, rel_bias):
    b, s, _ = qkv.shape
    h, e = HEADS_A, HEAD_DIM
    dilations = tuple(d for _, d in A_BRANCHES)
    for window, d in A_BRANCHES:
        assert window // (2 * d) == ATT_HALO and (s // d) % ATT_TQ == 0
    assert (s // ATT_TQ) % ATT_UNROLL == 0 and s % ATT_OUT_ROWS == 0
    chunk = min(ATT_CHUNK, s)
    assert s % chunk == 0 and chunk % ATT_OUT_ROWS == 0 and all((chunk // d) % 16 == 0 for d in dilations)
    win = ATT_TQ + 2 * ATT_HALO
    pad_rows = s + 2 * ATT_HALO * max(dilations)

    def in_spec(which):
        return pl.BlockSpec((1, s, e), lambda hi, bi: (bi, 0, which * h + hi))

    return pl.pallas_call(
        functools.partial(_attn_kernel, scale=1.0 / math.sqrt(e), dilations=dilations),
        grid=(h, b),
        in_specs=[pl.BlockSpec((1, len(dilations), 4, ATT_TQ, win), lambda hi, bi: (hi, 0, 0, 0, 0)),
                  in_spec(0), in_spec(1), in_spec(2)],
        out_specs=pl.BlockSpec((1, s, e), lambda hi, bi: (bi, 0, hi)),
        out_shape=jax.ShapeDtypeStruct((b, s, h * e), _BF16),
        scratch_shapes=[pltpu.VMEM((chunk, e), _F32), pltpu.VMEM((s, e), _BF16),
                        pltpu.VMEM((pad_rows, e), _BF16), pltpu.VMEM((pad_rows, e), _BF16),
                        pltpu.VMEM((s, e), _F32), pltpu.VMEM((s, LANES), _F32), pltpu.VMEM((s, LANES), _F32)],
        compiler_params=_params(("parallel", "parallel"), 56),
        name="dilated_attention",
    )(_attn_bias_tiles(rel_bias), qkv, qkv, qkv)


_HI = lax.Precision.HIGHEST


def _fft_stage1_kernel(f_ref, u_ref, ar_ref, ai_ref):
    n1 = u_ref.shape[1]
    r = jnp.dot(f_ref[...], u_ref[0], precision=_HI, preferred_element_type=_F32)
    ar_ref[0] = r[:n1]
    ai_ref[0] = r[n1:]


def _fft_stage2_kernel(twc_ref, tws_ref, c2_ref, s2_ref, cc_ref, sc_ref, ar_ref, ai_ref, o_ref):
    dot = functools.partial(jnp.dot, precision=_HI, preferred_element_type=_F32)
    ar, ai = ar_ref[0], ai_ref[0]
    c, s = twc_ref[...], tws_ref[...]
    pr = ar * c + ai * s
    pi = ai * c - ar * s
    c2, s2 = c2_ref[...], s2_ref[...]
    zr = dot(c2, pr) + dot(s2, pi)
    zi = dot(c2, pi) - dot(s2, pr)
    gd = cc_ref.shape[0]
    for g in range(ar.shape[1] // gd):
        cols = slice(g * gd, (g + 1) * gd)
        f = dot(zr[:, cols], cc_ref[...]) + dot(zi[:, cols], sc_ref[...])
        o_ref[0, :, cols] = f.astype(o_ref.dtype)


def _fourier_mix(u):
    b, s, width = u.shape
    gd = FNET_GROUP_DIM
    n2 = 128
    n1 = s // n2
    assert n1 * n2 == s and n1 % 8 == 0

    def trig(n, rows, cols):
        ang = 2.0 * np.pi * ((np.arange(rows)[:, None] * np.arange(cols)[None, :]) % n) / n
        return np.cos(ang), np.sin(ang)

    c1, s1 = trig(n1, n1, n1)
    f1 = jnp.asarray(np.concatenate([c1, -s1], axis=0), _F32)
    tc, tsn = trig(s, n1, n2)
    twc = jnp.asarray(tc.reshape(s, 1), _F32)
    tws = jnp.asarray(tsn.reshape(s, 1), _F32)
    c2, s2 = trig(n2, n2, n2)
    norm = 1.0 / math.sqrt(s * gd)
    cc, sc = trig(gd, gd, gd)
    c2, s2 = jnp.asarray(c2, _F32), jnp.asarray(s2, _F32)
    cc, sc = jnp.asarray(cc * norm, _F32), jnp.asarray(sc * norm, _F32)

    lanes = n2 * width
    tl = _pick(lanes, (4096, 2048, 1024, 512, 256, 128))
    a_spec = pl.BlockSpec((1, n1, tl), lambda bi, j: (bi, 0, j))
    ar, ai = pl.pallas_call(
        _fft_stage1_kernel,
        grid=(b, lanes // tl),
        in_specs=[pl.BlockSpec((2 * n1, n1), lambda bi, j: (0, 0)), a_spec],
        out_specs=[a_spec, a_spec],
        out_shape=[jax.ShapeDtypeStruct((b, n1, lanes), _F32)] * 2,
        compiler_params=_params(("parallel", "parallel"), 32),
        name="fft_stage1",
    )(f1, u.reshape(b, n1, lanes))

    tw_spec = pl.BlockSpec((n2, 1), lambda bi, k1: (k1, 0))
    m_spec = pl.BlockSpec((n2, n2), lambda bi, k1: (0, 0))
    g_spec = pl.BlockSpec((gd, gd), lambda bi, k1: (0, 0))
    z_spec = pl.BlockSpec((1, n2, width), lambda bi, k1: (bi, k1, 0))
    out = pl.pallas_call(
        _fft_stage2_kernel,
        grid=(b, n1),
        in_specs=[tw_spec, tw_spec, m_spec, m_spec, g_spec, g_spec, z_spec, z_spec],
        out_specs=pl.BlockSpec((1, n2, width), lambda bi, k1: (bi, 0, k1)),
        out_shape=jax.ShapeDtypeStruct((b, n2, n1 * width), _BF16),
        compiler_params=_params(("parallel", "parallel"), 32),
        name="fft_stage2",
    )(twc, tws, c2, s2, cc, sc, ar.reshape(b, s, width), ai.reshape(b, s, width))
    return out.reshape(b, s, width)


CONV_HALO = 16
CONV_ROWS = 64


def _glu_conv_kernel(w_ref, b_ref, vc_ref, vp_ref, vn_ref, gc_ref, gp_ref, gn_ref, y_ref, ext, *, width):
    i, last = pl.program_id(1), pl.num_programs(1) - 1
    ts, tc = vc_ref.shape[1], vc_ref.shape[2]
    halo, pad = CONV_HALO, (width - 1) // 2

    def glu(v_ref, g_ref):
        return v_ref[0] * jax.nn.sigmoid(g_ref[0])

    ext[0:halo, :] = jnp.where(i > 0, glu(vp_ref, gp_ref), 0.0)
    ext[halo:halo + ts, :] = glu(vc_ref, gc_ref)
    ext[halo + ts:, :] = jnp.where(i < last, glu(vn_ref, gn_ref), 0.0)
    rows = min(CONV_ROWS, ts)
    for cb in range(tc // LANES):
        cols = slice(cb * LANES, (cb + 1) * LANES)
        for r0 in range(0, ts, rows):
            acc = jnp.broadcast_to(b_ref[:, cols], (rows, LANES))
            for t in range(width):
                start = r0 + halo - pad + t
                acc = acc + w_ref[t:t + 1, cols] * ext[start:start + rows, cols]
            y_ref[0, r0:r0 + rows, cols] = acc


def _glu_conv(proj, conv_w, conv_b):
    b, s, _ = proj.shape
    width, cdim = conv_w.shape
    assert (width - 1) // 2 <= CONV_HALO
    ts = _pick(s, (256, 128))
    tc = _pick(cdim, (512, 256, 128))
    ncb, hb = cdim // tc, ts // CONV_HALO
    nhalo = s // CONV_HALO
    w_pad = jnp.pad(conv_w, ((0, -width % 8), (0, 0)))

    def cur(off):
        return pl.BlockSpec((1, ts, tc), lambda bi, i, c: (bi, i, off * ncb + c))

    def prev(off):
        return pl.BlockSpec((1, CONV_HALO, tc), lambda bi, i, c: (bi, jnp.maximum(i * hb - 1, 0), off * ncb + c))

    def nxt(off):
        return pl.BlockSpec((1, CONV_HALO, tc),
                            lambda bi, i, c: (bi, jnp.minimum((i + 1) * hb, nhalo - 1), off * ncb + c))

    return pl.pallas_call(
        functools.partial(_glu_conv_kernel, width=width),
        grid=(b, s // ts, ncb),
        in_specs=[pl.BlockSpec((w_pad.shape[0], tc), lambda bi, i, c: (0, c)),
                  pl.BlockSpec((1, tc), lambda bi, i, c: (0, c)),
                  cur(0), prev(0), nxt(0), cur(1), prev(1), nxt(1)],
        out_specs=pl.BlockSpec((1, ts, tc), lambda bi, i, c: (bi, i, c)),
        out_shape=jax.ShapeDtypeStruct((b, s, cdim), _F32),
        scratch_shapes=[pltpu.VMEM((ts + 2 * CONV_HALO, tc), _F32)],
        compiler_params=_params(("parallel", "parallel", "parallel"), 32),
        name="glu_conv",
    )(w_pad, conv_b.reshape(1, cdim), proj, proj, proj, proj, proj, proj)


SHORT_HALO = 8


def _norm_gate_kernel(y_ref, lg_ref, lb_ref, w_ref, db_ref, cc_ref, cp_ref, cn_ref, hc_ref, hp_ref, hn_ref,
                      o_ref, ext, *, width):
    i, last = pl.program_id(1), pl.num_programs(1) - 1
    ts, cdim = y_ref.shape[1], y_ref.shape[2]
    ddim = db_ref.shape[2]
    halo, pad = SHORT_HALO, (width - 1) // 2
    rows = min(CONV_ROWS, ts)

    for r0 in range(0, ts, rows):
        y = y_ref[0, r0:r0 + rows, :]
        yc = y - jnp.mean(y, axis=-1, keepdims=True)
        inv = lax.rsqrt(jnp.mean(yc * yc, axis=-1, keepdims=True) + NORM_EPS)
        n = yc * inv * lg_ref[...] + lb_ref[...]
        o_ref[0, r0:r0 + rows, 0:cdim] = (n * jax.nn.sigmoid(n)).astype(o_ref.dtype)

    ext[0:halo, :] = jnp.where(i > 0, cp_ref[0] * hp_ref[0], 0.0)
    ext[halo:halo + ts, :] = cc_ref[0] * hc_ref[0]
    ext[halo + ts:, :] = jnp.where(i < last, cn_ref[0] * hn_ref[0], 0.0)
    for cb in range(ddim // LANES):
        cols = slice(cb * LANES, (cb + 1) * LANES)
        for r0 in range(0, ts, rows):
            acc = jnp.zeros((rows, LANES), _F32)
            for t in range(width):
                start = r0 + halo - pad + t
                acc = acc + w_ref[t:t + 1, cols] * ext[start:start + rows, cols]
            o_ref[0, r0:r0 + rows, cdim + cb * LANES:cdim + (cb + 1) * LANES] = (
                db_ref[0, r0:r0 + rows, cols] * acc).astype(o_ref.dtype)


def _norm_gate(y, proj, ln_g, ln_b, conv_d_w):
    b, s, cdim = y.shape
    width, ddim = conv_d_w.shape
    assert (width - 1) // 2 <= SHORT_HALO and cdim % ddim == 0
    ts = _pick(s, (256, 128))
    hb, nhalo = ts // SHORT_HALO, s // SHORT_HALO
    base = 2 * cdim // ddim
    w_pad = jnp.pad(conv_d_w, ((0, -width % 8), (0, 0)))

    def cur(off):
        return pl.BlockSpec((1, ts, ddim), lambda bi, i: (bi, i, base + off))

    def prev(off):
        return pl.BlockSpec((1, SHORT_HALO, ddim), lambda bi, i: (bi, jnp.maximum(i * hb - 1, 0), base + off))

    def nxt(off):
        return pl.BlockSpec((1, SHORT_HALO, ddim),
                            lambda bi, i: (bi, jnp.minimum((i + 1) * hb, nhalo - 1), base + off))

    vec = pl.BlockSpec((1, cdim), lambda bi, i: (0, 0))
    return pl.pallas_call(
        functools.partial(_norm_gate_kernel, width=width),
        grid=(b, s // ts),
        in_specs=[pl.BlockSpec((1, ts, cdim), lambda bi, i: (bi, i, 0)), vec, vec,
                  pl.BlockSpec((w_pad.shape[0], ddim), lambda bi, i: (0, 0)),
                  cur(0), cur(1), prev(1), nxt(1), cur(2), prev(2), nxt(2)],
        out_specs=pl.BlockSpec((1, ts, cdim + ddim), lambda bi, i: (bi, i, 0)),
        out_shape=jax.ShapeDtypeStruct((b, s, cdim + ddim), _BF16),
        scratch_shapes=[pltpu.VMEM((ts + 2 * SHORT_HALO, ddim), _F32)],
        compiler_params=_params(("parallel", "parallel"), 48),
        name="norm_gate",
    )(y, ln_g.reshape(1, cdim), ln_b.reshape(1, cdim), w_pad, proj, proj, proj, proj, proj, proj, proj)


def _mixer_ab(x, g, w_in, w_out, rel_bias, b, s):
    qkv_dim = HEADS_A * HEAD_DIM
    w_in = w_in.astype(_BF16)
    qkv = _norm_matmul(x, g, w_in[:, :3 * qkv_dim], _BF16, "in_ab_qkv")
    u = _norm_matmul(x, g, w_in[:, 3 * qkv_dim:], _F32, "in_ab_u")
    a_out = _dilated_mixture(qkv.reshape(b, s, 3 * qkv_dim), rel_bias).reshape(b * s, qkv_dim)
    b_out = _fourier_mix(u.reshape(b, s, -1)).reshape(b * s, -1)
    return _matmul_resid([a_out, b_out], w_out.astype(_BF16), x, 1.0, "out_ab")


def _mixer_cd(x, g, w_in, conv_c_w, conv_c_b, ln_c_g, ln_c_b, conv_d_w, w_out, b, s):
    proj = _norm_matmul(x, g, w_in.astype(_BF16), _F32, "in_cd").reshape(b, s, -1)
    y = _glu_conv(proj, conv_c_w, conv_c_b)
    mixed = _norm_gate(y, proj, ln_c_g, ln_c_b, conv_d_w).reshape(b * s, -1)
    return _matmul_resid([mixed], w_out.astype(_BF16), x, 1.0, "out_cd")


def kernel(x, ffn1_norm, ffn1_w_gate, ffn1_w_up, ffn1_w_down, mix_norm, ffn2_norm, ffn2_w_gate, ffn2_w_up,
           ffn2_w_down, rel_bias, w_in_ab, w_out_ab, w_in_cd, conv_c_w, conv_c_b, ln_c_g, ln_c_b, conv_d_w,
           w_out_cd, final_norm):
    b, s, d = x.shape
    x = x.reshape(b * s, d)
    for layer in range(ffn1_norm.shape[0]):
        x = _ffn(x, ffn1_norm, ffn1_w_gate, ffn1_w_up, ffn1_w_down, layer, f"ffn1_l{layer}")
        i = layer // 2
        if layer % 2 == 0:
            x = _mixer_ab(x, mix_norm[layer], w_in_ab[i], w_out_ab[i], rel_bias, b, s)
        else:
            x = _mixer_cd(x, mix_norm[layer], w_in_cd[i], conv_c_w[i], conv_c_b[i], ln_c_g[i], ln_c_b[i],
                          conv_d_w[i], w_out_cd[i], b, s)
        x = _ffn(x, ffn2_norm, ffn2_w_gate, ffn2_w_up, ffn2_w_down, layer, f"ffn2_l{layer}")
    return _rmsnorm(x, final_norm, _F32, "final_norm").reshape(b, s, d)
```

```python
import functools
import math

import jax
import jax.numpy as jnp
import numpy as np
from jax import lax
from jax.experimental import pallas as pl
from jax.experimental.pallas import tpu as pltpu

HEAD_DIM = 128
HEADS_A = 24
A_BRANCHES = ((128, 1), (512, 4), (2048, 16))
FNET_GROUPS = 8
FNET_GROUP_DIM = 128
REL_BUCKETS = 32
REL_MAX_DISTANCE = 1024
CONV_C_DIM = 2048
CONV_D_DIM = 2048
FFN_RESIDUAL = 0.5
NORM_EPS = 1e-6
NEG_INF = -1e30

LANES = 128
SUBLANES = 8
ATT_TQ = 128
ATT_HALO = 64
ATT_UNROLL = 16
ATT_OUT_ROWS = 256
ATT_CHUNK = 2048
MIB = 1024 * 1024

_F32 = jnp.float32
_BF16 = jnp.bfloat16


def _params(sem, vmem_mib):
    return pltpu.CompilerParams(dimension_semantics=sem, vmem_limit_bytes=vmem_mib * MIB)


def _pick(n, prefs):
    for p in prefs:
        if n % p == 0:
            return p
    return n


def _rms(x, g):
    inv = lax.rsqrt(jnp.mean(x * x, axis=-1, keepdims=True) + NORM_EPS)
    return x * inv * g


def _norm_matmul_kernel(x_ref, g_ref, w_ref, *refs, bounds):
    o_refs, h_ref = refs[:-1], refs[-1]
    j = pl.program_id(1)

    @pl.when(j == 0)
    def _():
        h_ref[...] = _rms(x_ref[...], g_ref[...]).astype(h_ref.dtype)

    def emit(o_ref):
        o_ref[...] = jnp.dot(h_ref[...], w_ref[...], preferred_element_type=_F32).astype(o_ref.dtype)

    if len(o_refs) == 1:
        emit(o_refs[0])
    for o_ref, (lo, hi) in zip(o_refs if len(o_refs) > 1 else (), bounds):
        pl.when(jnp.logical_and(j >= lo, j < hi))(functools.partial(emit, o_ref))


def _norm_matmul(x, g, w, outs, name):
    m, d = x.shape
    n = w.shape[1]
    tm = _pick(m, (512, 256, 128))
    tn = _pick(math.gcd(*[wd for wd, _ in outs]), (1024, 512, 256, 128))
    bounds, specs, col = [], [], 0
    for wd, _ in outs:
        lo, hi = col // tn, (col + wd) // tn
        bounds.append((lo, hi))
        specs.append(pl.BlockSpec((tm, tn), lambda i, j, lo=lo, hi=hi: (i, jnp.clip(j - lo, 0, hi - lo - 1))))
        col += wd
    assert col == n
    res = pl.pallas_call(
        functools.partial(_norm_matmul_kernel, bounds=tuple(bounds)),
        grid=(m // tm, n // tn),
        in_specs=[pl.BlockSpec((tm, d), lambda i, j: (i, 0)),
                  pl.BlockSpec((1, d), lambda i, j: (0, 0)),
                  pl.BlockSpec((d, tn), lambda i, j: (0, j))],
        out_specs=specs,
        out_shape=[jax.ShapeDtypeStruct((m, wd), dt) for wd, dt in outs],
        scratch_shapes=[pltpu.VMEM((tm, d), _BF16)],
        compiler_params=_params(("parallel", "arbitrary"), 56),
        name=name,
    )(x, g.reshape(1, d), w)
    return res


def _rmsnorm_kernel(x_ref, g_ref, o_ref):
    o_ref[...] = _rms(x_ref[...], g_ref[...]).astype(o_ref.dtype)


def _rmsnorm(x, g, out_dtype, name):
    m, d = x.shape
    tm = _pick(m, (256, 128))
    return pl.pallas_call(
        _rmsnorm_kernel,
        grid=(m // tm,),
        in_specs=[pl.BlockSpec((tm, d), lambda i: (i, 0)), pl.BlockSpec((1, d), lambda i: (0, 0))],
        out_specs=pl.BlockSpec((tm, d), lambda i: (i, 0)),
        out_shape=jax.ShapeDtypeStruct((m, d), out_dtype),
        compiler_params=_params(("parallel",), 32),
        name=name,
    )(x, g.reshape(1, d))


def _matmul_resid_kernel(*refs, n, scale):
    a_refs, w_refs, x_ref, o_ref = refs[:n], refs[n:2 * n], refs[2 * n], refs[2 * n + 1]
    acc = jnp.dot(a_refs[0][...], w_refs[0][...], preferred_element_type=_F32)
    for a_ref, w_ref in zip(a_refs[1:], w_refs[1:]):
        acc = acc + jnp.dot(a_ref[...], w_ref[...], preferred_element_type=_F32)
    o_ref[...] = x_ref[...] + (acc if scale == 1.0 else scale * acc)


def _matmul_resid(pieces, w, x, scale, name):
    m, n = x.shape
    ktot = w.shape[0]
    tm = _pick(m, (1024, 512, 256, 128)) if ktot <= 4096 else _pick(m, (512, 256, 128))
    tn = _pick(n, (512, 256, 128))
    a_specs, w_specs, row = [], [], 0
    for a in pieces:
        kp = a.shape[1]
        assert row % kp == 0
        a_specs.append(pl.BlockSpec((tm, kp), lambda i, j: (i, 0)))
        w_specs.append(pl.BlockSpec((kp, tn), lambda i, j, rb=row // kp: (rb, j)))
        row += kp
    assert row == ktot
    xo_spec = pl.BlockSpec((tm, tn), lambda i, j: (i, j))
    return pl.pallas_call(
        functools.partial(_matmul_resid_kernel, n=len(pieces), scale=scale),
        grid=(m // tm, n // tn),
        in_specs=a_specs + w_specs + [xo_spec],
        out_specs=xo_spec,
        out_shape=jax.ShapeDtypeStruct((m, n), _F32),
        compiler_params=_params(("parallel", "parallel"), 56),
        name=name,
    )(*pieces, *([w] * len(pieces)), x)


def _ffn_up_kernel(h_ref, wg_ref, wu_ref, wd_ref, o_ref, wdo_ref, wcat):
    tn = wg_ref.shape[2]

    @pl.when(pl.program_id(1) == 0)
    def _():
        wcat[:, :tn] = wg_ref[0].astype(wcat.dtype)
        wcat[:, tn:] = wu_ref[0].astype(wcat.dtype)
        wdo_ref[...] = wd_ref[0].astype(wdo_ref.dtype)

    r = jnp.dot(h_ref[...], wcat[...], preferred_element_type=_F32)
    a, b = r[:, :tn], r[:, tn:]
    o_ref[...] = (a * jax.nn.sigmoid(a) * b).astype(o_ref.dtype)


def _ffn_up(h, w_gate, w_up, w_down, layer, name):
    m, d = h.shape
    f = w_gate.shape[2]
    tm = _pick(m, (1024, 512, 256, 128))
    tn = _pick(f, (256, 128))
    w_spec = pl.BlockSpec((1, d, tn), lambda j, i: (layer, 0, j))
    return pl.pallas_call(
        _ffn_up_kernel,
        grid=(f // tn, m // tm),
        in_specs=[pl.BlockSpec((tm, d), lambda j, i: (i, 0)), w_spec, w_spec,
                  pl.BlockSpec((1, tn, d), lambda j, i: (layer, j, 0))],
        out_specs=[pl.BlockSpec((tm, tn), lambda j, i: (i, j)),
                   pl.BlockSpec((tn, d), lambda j, i: (j, 0))],
        out_shape=[jax.ShapeDtypeStruct((m, f), _BF16), jax.ShapeDtypeStruct((f, d), _BF16)],
        scratch_shapes=[pltpu.VMEM((d, 2 * tn), _BF16)],
        compiler_params=_params(("arbitrary", "arbitrary"), 56),
        name=name,
    )(h, w_gate, w_up, w_down)


def _ffn(x, g, w_gate, w_up, w_down, layer, name):
    h = _rmsnorm(x, g[layer], _BF16, name + "_norm")
    hidden, wd = _ffn_up(h, w_gate, w_up, w_down, layer, name + "_up")
    return _matmul_resid([hidden], wd, x, FFN_RESIDUAL, name + "_down")


def _t5_bucket_np(rel):
    half = REL_BUCKETS // 2
    max_exact = half // 2
    ret = np.where(rel > 0, half, 0)
    n = np.abs(rel)
    nf = np.maximum(n, 1).astype(np.float64)
    val = np.log(nf / max_exact) / math.log(REL_MAX_DISTANCE / max_exact) * (half - max_exact)
    large = np.minimum(max_exact + val.astype(np.int64), half - 1)
    return (ret + np.where(n < max_exact, n, large)).astype(np.int32)


def _branch_dilations():
    return tuple(sorted((d for _, d in A_BRANCHES), reverse=True))


def _attn_bias_tiles(rel_bias):
    tq, halo = ATT_TQ, ATT_HALO
    win = tq + 2 * halo
    period = 2 * win
    c = np.arange(win)[None, :]
    ok_first = np.broadcast_to(c >= halo, (tq, win))
    ok_last = np.broadcast_to(c < tq + halo, (tq, win))
    branches = []
    for dilation in _branch_dilations():
        rel = np.arange(-halo, halo + 1)
        band = rel_bias[_t5_bucket_np(rel * dilation)].astype(_F32).T
        heads = band.shape[0]
        row = jnp.full((heads, period), NEG_INF, _F32).at[:, :2 * halo + 1].set(band)
        toeplitz = jnp.tile(row, (1, tq))[:, :tq * (period - 1)].reshape(heads, tq, period - 1)[:, :, :win]
        branches.append(jnp.stack(
            [jnp.where(mk[None], toeplitz, NEG_INF)
             for mk in (np.ones((tq, win), bool), ok_first, ok_last, ok_first & ok_last)], axis=1))
    return jnp.stack(branches, axis=1)


def _attn_kernel(bm_ref, q_ref, k_ref, v_ref, o_ref, wide, qd, kd, vd, acc, mrun, lrun, *, scale, dilations):
    seq = q_ref.shape[1]
    tq, halo = ATT_TQ, ATT_HALO
    win = tq + 2 * halo
    ntiles = seq // tq
    zeros = jnp.zeros((halo, HEAD_DIM), kd.dtype)
    ones = jnp.ones((win, LANES), vd.dtype)

    for g, d in enumerate(dilations):
        sub = seq // d
        nt = sub // tq
        chunk = wide.shape[0]
        per = chunk // d
        for src, dst, padded in ((q_ref, qd, False), (k_ref, kd, True), (v_ref, vd, True)):
            if d == 1 and not padded:
                continue
            stride_r = sub + 2 * halo if padded else sub
            first = halo if padded else 0
            for c in range(seq // chunk):
                def move(i, carry, c=c, src=src, dst=dst, first=first, widen=d > 1):
                    off = pl.multiple_of(i * ATT_OUT_ROWS, ATT_OUT_ROWS)
                    rows = src[0, pl.ds(c * chunk + off, ATT_OUT_ROWS), :]
                    if widen:
                        wide[pl.ds(off, ATT_OUT_ROWS), :] = rows.astype(_F32)
                    else:
                        dst[pl.ds(pl.multiple_of(first + c * chunk + off, ATT_HALO), ATT_OUT_ROWS), :] = rows
                    return carry

                lax.fori_loop(0, chunk // ATT_OUT_ROWS, move, 0)
                if d == 1:
                    continue
                for r in range(d):
                    base = r * stride_r + first + c * per
                    dst[base:base + per, :] = wide[pl.ds(r, per, stride=d), :].astype(dst.dtype)
            if padded:
                for r in range(d):
                    dst[r * stride_r:r * stride_r + halo, :] = zeros
                    dst[r * stride_r + halo + sub:(r + 1) * stride_r, :] = zeros

        def tile(tile_idx, g=g, d=d, nt=nt):
            r = tile_idx // nt
            t = tile_idx - r * nt
            q0 = pl.multiple_of(tile_idx * tq, tq)
            k0 = pl.multiple_of(tile_idx * tq + r * (2 * halo), 2 * halo)
            q = q_ref[0, pl.ds(q0, tq), :] if d == 1 else qd[pl.ds(q0, tq), :]
            kw = kd[pl.ds(k0, win), :]
            vw = vd[pl.ds(k0, win), :]
            s = lax.dot_general(q, kw, (((1,), (1,)), ((), ())), preferred_element_type=_F32)
            variant = jnp.where(t == 0, 1, 0) + jnp.where(t == nt - 1, 2, 0)
            s = s * scale + bm_ref[0, g, variant]
            mx = jnp.max(s, axis=-1, keepdims=True)
            p = jnp.exp(s - mx)
            pv = jnp.dot(p.astype(vw.dtype), jnp.concatenate([vw, ones], axis=1), preferred_element_type=_F32)
            pv, den = pv[:, :HEAD_DIM], pv[:, HEAD_DIM:]
            idx = pl.ds(q0, tq) if d == 1 else pl.ds(t * (tq * d) + r, tq, stride=d)
            if g == 0:
                acc[idx, :] = pv
                mrun[idx, :] = jnp.broadcast_to(mx, (tq, LANES))
                lrun[idx, :] = den
            else:
                m_old = mrun[idx, :]
                m_new = jnp.maximum(m_old, mx)
                c_old = jnp.exp(m_old - m_new)
                c_new = jnp.exp(mx - m_new)
                acc[idx, :] = acc[idx, :] * c_old + pv * c_new
                lrun[idx, :] = lrun[idx, :] * c_old + den * c_new
                mrun[idx, :] = m_new

        def tiles(it, carry, tile=tile):
            for u in range(ATT_UNROLL):
                tile(it * ATT_UNROLL + u)
            return carry

        lax.fori_loop(0, ntiles // ATT_UNROLL, tiles, 0)

    def finish(it, carry):
        rows = pl.ds(pl.multiple_of(it * ATT_OUT_ROWS, ATT_OUT_ROWS), ATT_OUT_ROWS)
        o_ref[0, rows, :] = (acc[rows, :] / lrun[rows, :]).astype(o_ref.dtype)
        return carry

    lax.fori_loop(0, seq // ATT_OUT_ROWS, finish, 0)


def _dilated_mixture(qkv, rel_bias):
    b, s, _ = qkv.shape
    h, e = HEADS_A, HEAD_DIM
    dilations = _branch_dilations()
    for window, d in A_BRANCHES:
        assert window // (2 * d) == ATT_HALO and (s // d) % ATT_TQ == 0
    assert (s // ATT_TQ) % ATT_UNROLL == 0 and s % ATT_OUT_ROWS == 0
    chunk = min(ATT_CHUNK, s)
    assert s % chunk == 0 and chunk % ATT_OUT_ROWS == 0 and all((chunk // d) % 16 == 0 for d in dilations)
    win = ATT_TQ + 2 * ATT_HALO
    pad_rows = s + 2 * ATT_HALO * max(dilations)

    def in_spec(which):
        return pl.BlockSpec((1, s, e), lambda hi, bi: (bi, 0, which * h + hi))

    return pl.pallas_call(
        functools.partial(_attn_kernel, scale=1.0 / math.sqrt(e), dilations=dilations),
        grid=(h, b),
        in_specs=[pl.BlockSpec((1, len(dilations), 4, ATT_TQ, win), lambda hi, bi: (hi, 0, 0, 0, 0)),
                  in_spec(0), in_spec(1), in_spec(2)],
        out_specs=pl.BlockSpec((1, s, e), lambda hi, bi: (bi, 0, hi)),
        out_shape=jax.ShapeDtypeStruct((b, s, h * e), _BF16),
        scratch_shapes=[pltpu.VMEM((chunk, e), _F32), pltpu.VMEM((s, e), _BF16),
                        pltpu.VMEM((pad_rows, e), _BF16), pltpu.VMEM((pad_rows, e), _BF16),
                        pltpu.VMEM((s, e), _F32), pltpu.VMEM((s, LANES), _F32), pltpu.VMEM((s, LANES), _F32)],
        compiler_params=_params(("parallel", "parallel"), 56),
        name="dilated_attention",
    )(_attn_bias_tiles(rel_bias), qkv, qkv, qkv)


_HI = lax.Precision.HIGHEST


def _fft_stage1_kernel(f_ref, u_ref, ar_ref, ai_ref):
    n1 = u_ref.shape[1]
    r = jnp.dot(f_ref[...], u_ref[0], precision=_HI, preferred_element_type=_F32)
    ar_ref[0] = r[:n1]
    ai_ref[0] = r[n1:]


def _fft_stage2_kernel(twc_ref, tws_ref, c2_ref, s2_ref, cc_ref, sc_ref, ar_ref, ai_ref, o_ref):
    dot = functools.partial(jnp.dot, precision=_HI, preferred_element_type=_F32)
    ar, ai = ar_ref[0], ai_ref[0]
    c, s = twc_ref[...], tws_ref[...]
    pr = ar * c + ai * s
    pi = ai * c - ar * s
    c2, s2 = c2_ref[...], s2_ref[...]
    zr = dot(c2, pr) + dot(s2, pi)
    zi = dot(c2, pi) - dot(s2, pr)
    gd = cc_ref.shape[0]
    for g in range(ar.shape[1] // gd):
        cols = slice(g * gd, (g + 1) * gd)
        f = dot(zr[:, cols], cc_ref[...]) + dot(zi[:, cols], sc_ref[...])
        o_ref[0, :, cols] = f.astype(o_ref.dtype)


def _fourier_mix(u):
    b, s, width = u.shape
    gd = FNET_GROUP_DIM
    n2 = 128
    n1 = s // n2
    assert n1 * n2 == s and n1 % 8 == 0

    def trig(n, rows, cols):
        ang = 2.0 * np.pi * ((np.arange(rows)[:, None] * np.arange(cols)[None, :]) % n) / n
        return np.cos(ang), np.sin(ang)

    c1, s1 = trig(n1, n1, n1)
    f1 = jnp.asarray(np.concatenate([c1, -s1], axis=0), _F32)
    tc, tsn = trig(s, n1, n2)
    twc = jnp.asarray(tc.reshape(s, 1), _F32)
    tws = jnp.asarray(tsn.reshape(s, 1), _F32)
    c2, s2 = trig(n2, n2, n2)
    norm = 1.0 / math.sqrt(s * gd)
    cc, sc = trig(gd, gd, gd)
    c2, s2 = jnp.asarray(c2, _F32), jnp.asarray(s2, _F32)
    cc, sc = jnp.asarray(cc * norm, _F32), jnp.asarray(sc * norm, _F32)

    lanes = n2 * width
    tl = _pick(lanes, (4096, 2048, 1024, 512, 256, 128))
    a_spec = pl.BlockSpec((1, n1, tl), lambda bi, j: (bi, 0, j))
    ar, ai = pl.pallas_call(
        _fft_stage1_kernel,
        grid=(b, lanes // tl),
        in_specs=[pl.BlockSpec((2 * n1, n1), lambda bi, j: (0, 0)), a_spec],
        out_specs=[a_spec, a_spec],
        out_shape=[jax.ShapeDtypeStruct((b, n1, lanes), _F32)] * 2,
        compiler_params=_params(("parallel", "parallel"), 32),
        name="fft_stage1",
    )(f1, u.reshape(b, n1, lanes))

    tw_spec = pl.BlockSpec((n2, 1), lambda bi, k1: (k1, 0))
    m_spec = pl.BlockSpec((n2, n2), lambda bi, k1: (0, 0))
    g_spec = pl.BlockSpec((gd, gd), lambda bi, k1: (0, 0))
    z_spec = pl.BlockSpec((1, n2, width), lambda bi, k1: (bi, k1, 0))
    out = pl.pallas_call(
        _fft_stage2_kernel,
        grid=(b, n1),
        in_specs=[tw_spec, tw_spec, m_spec, m_spec, g_spec, g_spec, z_spec, z_spec],
        out_specs=pl.BlockSpec((1, n2, width), lambda bi, k1: (bi, 0, k1)),
        out_shape=jax.ShapeDtypeStruct((b, n2, n1 * width), _BF16),
        compiler_params=_params(("parallel", "parallel"), 32),
        name="fft_stage2",
    )(twc, tws, c2, s2, cc, sc, ar.reshape(b, s, width), ai.reshape(b, s, width))
    return out.reshape(b, s, width)


CONV_HALO = 16
CONV_ROWS = 64


def _glu_conv_kernel(w_ref, b_ref, vc_ref, vp_ref, vn_ref, gc_ref, gp_ref, gn_ref, y_ref, ext, shifted, *, width):
    i, last = pl.program_id(1), pl.num_programs(1) - 1
    ts, tc = vc_ref.shape[1], vc_ref.shape[2]
    halo, pad = CONV_HALO, (width - 1) // 2

    def glu(v_ref, g_ref):
        return v_ref[0] * jax.nn.sigmoid(g_ref[0])

    ext[0:halo, :] = jnp.where(i > 0, glu(vp_ref, gp_ref), 0.0)
    ext[halo:halo + ts, :] = glu(vc_ref, gc_ref)
    ext[halo + ts:, :] = jnp.where(i < last, glu(vn_ref, gn_ref), 0.0)
    rows = min(CONV_ROWS, ts)
    span = shifted.shape[1]
    for cb in range(tc // LANES):
        cols = slice(cb * LANES, (cb + 1) * LANES)
        for k in range(1, SUBLANES):
            shifted[k - 1, :, cols] = ext[k:k + span, cols]
        for r0 in range(0, ts, rows):
            acc = jnp.broadcast_to(b_ref[:, cols], (rows, LANES))
            for t in range(width):
                start = r0 + halo - pad + t
                k, base = start % SUBLANES, start - start % SUBLANES
                tap = ext[base:base + rows, cols] if k == 0 else shifted[k - 1, base:base + rows, cols]
                acc = acc + w_ref[t:t + 1, cols] * tap
            y_ref[0, r0:r0 + rows, cols] = acc


def _glu_conv(proj, conv_w, conv_b):
    b, s, _ = proj.shape
    width, cdim = conv_w.shape
    assert (width - 1) // 2 <= CONV_HALO
    ts = _pick(s, (256, 128))
    tc = _pick(cdim, (512, 256, 128))
    ncb, hb = cdim // tc, ts // CONV_HALO
    nhalo = s // CONV_HALO
    w_pad = jnp.pad(conv_w, ((0, -width % 8), (0, 0)))

    def cur(off):
        return pl.BlockSpec((1, ts, tc), lambda bi, i, c: (bi, i, off * ncb + c))

    def prev(off):
        return pl.BlockSpec((1, CONV_HALO, tc), lambda bi, i, c: (bi, jnp.maximum(i * hb - 1, 0), off * ncb + c))

    def nxt(off):
        return pl.BlockSpec((1, CONV_HALO, tc),
                            lambda bi, i, c: (bi, jnp.minimum((i + 1) * hb, nhalo - 1), off * ncb + c))

    return pl.pallas_call(
        functools.partial(_glu_conv_kernel, width=width),
        grid=(b, s // ts, ncb),
        in_specs=[pl.BlockSpec((w_pad.shape[0], tc), lambda bi, i, c: (0, c)),
                  pl.BlockSpec((1, tc), lambda bi, i, c: (0, c)),
                  cur(0), prev(0), nxt(0), cur(1), prev(1), nxt(1)],
        out_specs=pl.BlockSpec((1, ts, tc), lambda bi, i, c: (bi, i, c)),
        out_shape=jax.ShapeDtypeStruct((b, s, cdim), _F32),
        scratch_shapes=[pltpu.VMEM((ts + 2 * CONV_HALO, tc), _F32),
                        pltpu.VMEM((SUBLANES - 1, ts + 2 * CONV_HALO - SUBLANES, tc), _F32)],
        compiler_params=_params(("parallel", "parallel", "parallel"), 32),
        name="glu_conv",
    )(w_pad, conv_b.reshape(1, cdim), proj, proj, proj, proj, proj, proj)


SHORT_HALO = 8


def _norm_gate_kernel(y_ref, lg_ref, lb_ref, w_ref, db_ref, cc_ref, cp_ref, cn_ref, hc_ref, hp_ref, hn_ref,
                      o_ref, ext, *, width):
    i, last = pl.program_id(1), pl.num_programs(1) - 1
    ts, cdim = y_ref.shape[1], y_ref.shape[2]
    ddim = db_ref.shape[2]
    halo, pad = SHORT_HALO, (width - 1) // 2
    rows = min(CONV_ROWS, ts)

    for r0 in range(0, ts, rows):
        y = y_ref[0, r0:r0 + rows, :]
        yc = y - jnp.mean(y, axis=-1, keepdims=True)
        inv = lax.rsqrt(jnp.mean(yc * yc, axis=-1, keepdims=True) + NORM_EPS)
        n = yc * inv * lg_ref[...] + lb_ref[...]
        o_ref[0, r0:r0 + rows, 0:cdim] = (n * jax.nn.sigmoid(n)).astype(o_ref.dtype)

    ext[0:halo, :] = jnp.where(i > 0, cp_ref[0] * hp_ref[0], 0.0)
    ext[halo:halo + ts, :] = cc_ref[0] * hc_ref[0]
    ext[halo + ts:, :] = jnp.where(i < last, cn_ref[0] * hn_ref[0], 0.0)
    for cb in range(ddim // LANES):
        cols = slice(cb * LANES, (cb + 1) * LANES)
        for r0 in range(0, ts, rows):
            acc = jnp.zeros((rows, LANES), _F32)
            for t in range(width):
                start = r0 + halo - pad + t
                acc = acc + w_ref[t:t + 1, cols] * ext[start:start + rows, cols]
            o_ref[0, r0:r0 + rows, cdim + cb * LANES:cdim + (cb + 1) * LANES] = (
                db_ref[0, r0:r0 + rows, cols] * acc).astype(o_ref.dtype)


def _norm_gate(y, proj, ln_g, ln_b, conv_d_w):
    b, s, cdim = y.shape
    width, ddim = conv_d_w.shape
    assert (width - 1) // 2 <= SHORT_HALO and cdim % ddim == 0
    ts = _pick(s, (256, 128))
    hb, nhalo = ts // SHORT_HALO, s // SHORT_HALO
    base = 2 * cdim // ddim
    w_pad = jnp.pad(conv_d_w, ((0, -width % 8), (0, 0)))

    def cur(off):
        return pl.BlockSpec((1, ts, ddim), lambda bi, i: (bi, i, base + off))

    def prev(off):
        return pl.BlockSpec((1, SHORT_HALO, ddim), lambda bi, i: (bi, jnp.maximum(i * hb - 1, 0), base + off))

    def nxt(off):
        return pl.BlockSpec((1, SHORT_HALO, ddim),
                            lambda bi, i: (bi, jnp.minimum((i + 1) * hb, nhalo - 1), base + off))

    vec = pl.BlockSpec((1, cdim), lambda bi, i: (0, 0))
    return pl.pallas_call(
        functools.partial(_norm_gate_kernel, width=width),
        grid=(b, s // ts),
        in_specs=[pl.BlockSpec((1, ts, cdim), lambda bi, i: (bi, i, 0)), vec, vec,
                  pl.BlockSpec((w_pad.shape[0], ddim), lambda bi, i: (0, 0)),
                  cur(0), cur(1), prev(1), nxt(1), cur(2), prev(2), nxt(2)],
        out_specs=pl.BlockSpec((1, ts, cdim + ddim), lambda bi, i: (bi, i, 0)),
        out_shape=jax.ShapeDtypeStruct((b, s, cdim + ddim), _BF16),
        scratch_shapes=[pltpu.VMEM((ts + 2 * SHORT_HALO, ddim), _F32)],
        compiler_params=_params(("parallel", "parallel"), 48),
        name="norm_gate",
    )(y, ln_g.reshape(1, cdim), ln_b.reshape(1, cdim), w_pad, proj, proj, proj, proj, proj, proj, proj)


def _mixer_ab(x, g, w_in, w_out, rel_bias, b, s):
    qkv_dim = HEADS_A * HEAD_DIM
    outs = ((3 * qkv_dim, _BF16), (w_in.shape[1] - 3 * qkv_dim, _F32))
    qkv, u = _norm_matmul(x, g, w_in.astype(_BF16), outs, "in_ab")
    a_out = _dilated_mixture(qkv.reshape(b, s, 3 * qkv_dim), rel_bias).reshape(b * s, qkv_dim)
    b_out = _fourier_mix(u.reshape(b, s, -1)).reshape(b * s, -1)
    return _matmul_resid([a_out, b_out], w_out.astype(_BF16), x, 1.0, "out_ab")


def _mixer_cd(x, g, w_in, conv_c_w, conv_c_b, ln_c_g, ln_c_b, conv_d_w, w_out, b, s):
    (proj,) = _norm_matmul(x, g, w_in.astype(_BF16), ((w_in.shape[1], _F32),), "in_cd")
    proj = proj.reshape(b, s, -1)
    y = _glu_conv(proj, conv_c_w, conv_c_b)
    mixed = _norm_gate(y, proj, ln_c_g, ln_c_b, conv_d_w).reshape(b * s, -1)
    return _matmul_resid([mixed], w_out.astype(_BF16), x, 1.0, "out_cd")


def kernel(x, ffn1_norm, ffn1_w_gate, ffn1_w_up, ffn1_w_down, mix_norm, ffn2_norm, ffn2_w_gate, ffn2_w_up,
           ffn2_w_down, rel_bias, w_in_ab, w_out_ab, w_in_cd, conv_c_w, conv_c_b, ln_c_g, ln_c_b, conv_d_w,
           w_out_cd, final_norm):
    b, s, d = x.shape
    x = x.reshape(b * s, d)
    for layer in range(ffn1_norm.shape[0]):
        x = _ffn(x, ffn1_norm, ffn1_w_gate, ffn1_w_up, ffn1_w_down, layer, f"ffn1_l{layer}")
        i = layer // 2
        if layer % 2 == 0:
            x = _mixer_ab(x, mix_norm[layer], w_in_ab[i], w_out_ab[i], rel_bias, b, s)
        else:
            x = _mixer_cd(x, mix_norm[layer], w_in_cd[i], conv_c_w[i], conv_c_b[i], ln_c_g[i], ln_c_b[i],
                          conv_d_w[i], w_out_cd[i], b, s)
        x = _ffn(x, ffn2_norm, ffn2_w_gate, ffn2_w_up, ffn2_w_down, layer, f"ffn2_l{layer}")
    return _rmsnorm(x, final_norm, _F32, "final_norm").reshape(b, s, d)
```

```python
import functools
import math

import jax
import jax.numpy as jnp
import numpy as np
from jax import lax
from jax.experimental import pallas as pl
from jax.experimental.pallas import tpu as pltpu

HEAD_DIM = 128
HEADS_A = 24
A_BRANCHES = ((128, 1), (512, 4), (2048, 16))
FNET_GROUPS = 8
FNET_GROUP_DIM = 128
REL_BUCKETS = 32
REL_MAX_DISTANCE = 1024
CONV_C_DIM = 2048
CONV_D_DIM = 2048
FFN_RESIDUAL = 0.5
NORM_EPS = 1e-6
NEG_INF = -1e30

LANES = 128
SUBLANES = 8
ATT_TQ = 128
ATT_HALO = 64
ATT_UNROLL = 16
ATT_OUT_ROWS = 256
ATT_CHUNK = 2048
MIB = 1024 * 1024

_F32 = jnp.float32
_BF16 = jnp.bfloat16


def _params(sem, vmem_mib):
    return pltpu.CompilerParams(dimension_semantics=sem, vmem_limit_bytes=vmem_mib * MIB)


def _pick(n, prefs):
    for p in prefs:
        if n % p == 0:
            return p
    return n


def _rms(x, g):
    inv = lax.rsqrt(jnp.mean(x * x, axis=-1, keepdims=True) + NORM_EPS)
    return x * inv * g


def _col_to_row(col):
    parts = [jnp.transpose(jnp.broadcast_to(col[k:k + LANES], (LANES, LANES)))[0:1, :]
             for k in range(0, col.shape[0], LANES)]
    return parts[0] if len(parts) == 1 else jnp.concatenate(parts, axis=1)


def _inv_rms(ss_ref, d):
    tot = ss_ref[0]
    for k in range(1, ss_ref.shape[0]):
        tot = tot + ss_ref[k]
    return lax.rsqrt(tot / d + NORM_EPS)


def _row_scale(r, inv_row):
    reps = r.shape[1] // LANES
    blocks = []
    for k in range(0, r.shape[0], LANES):
        bc = jnp.transpose(jnp.broadcast_to(inv_row[:, k:k + LANES], (LANES, LANES)))
        blocks.append(r[k:k + LANES] * (bc if reps == 1 else jnp.concatenate([bc] * reps, axis=1)))
    return blocks[0] if len(blocks) == 1 else jnp.concatenate(blocks, axis=0)


def _rmsnorm_kernel(x_ref, g_ref, o_ref):
    o_ref[...] = _rms(x_ref[...], g_ref[...]).astype(o_ref.dtype)


def _rmsnorm(x, g, out_dtype, name):
    m, d = x.shape
    tm = _pick(m, (256, 128))
    return pl.pallas_call(
        _rmsnorm_kernel,
        grid=(m // tm,),
        in_specs=[pl.BlockSpec((tm, d), lambda i: (i, 0)), pl.BlockSpec((1, d), lambda i: (0, 0))],
        out_specs=pl.BlockSpec((tm, d), lambda i: (i, 0)),
        out_shape=jax.ShapeDtypeStruct((m, d), out_dtype),
        compiler_params=_params(("parallel",), 32),
        name=name,
    )(x, g.reshape(1, d))


def _scaled_matmul_kernel(h_ref, ss_ref, w_ref, o_ref, wbf):
    @pl.when(pl.program_id(1) == 0)
    def _():
        wbf[...] = w_ref[0].astype(wbf.dtype)

    r = jnp.dot(h_ref[...], wbf[...], preferred_element_type=_F32)
    o_ref[...] = _row_scale(r, _inv_rms(ss_ref, h_ref.shape[1])).astype(o_ref.dtype)


def _scaled_matmul(h, stats, w, index, col0, width, out_dtype, name):
    m, d = h.shape
    tm = _pick(m, (1024, 512, 256, 128))
    tn = _pick(math.gcd(col0, width), (512, 256, 128))
    return pl.pallas_call(
        _scaled_matmul_kernel,
        grid=(width // tn, m // tm),
        in_specs=[pl.BlockSpec((tm, d), lambda j, i: (i, 0)),
                  pl.BlockSpec((stats.shape[0], 1, tm), lambda j, i: (0, 0, i)),
                  pl.BlockSpec((1, d, tn), lambda j, i: (index, 0, col0 // tn + j))],
        out_specs=pl.BlockSpec((tm, tn), lambda j, i: (i, j)),
        out_shape=jax.ShapeDtypeStruct((m, width), out_dtype),
        scratch_shapes=[pltpu.VMEM((d, tn), _BF16)],
        compiler_params=_params(("arbitrary", "arbitrary"), 56),
        name=name,
    )(h, stats, w)


def _matmul_resid_kernel(*refs, n, scale, emit_norm):
    a_refs, w_refs, x_ref = refs[:n], refs[n:2 * n], refs[2 * n]
    acc = jnp.dot(a_refs[0][...], w_refs[0][...], preferred_element_type=_F32)
    for a_ref, w_ref in zip(a_refs[1:], w_refs[1:]):
        acc = acc + jnp.dot(a_ref[...], w_ref[...], preferred_element_type=_F32)
    y = x_ref[...] + (acc if scale == 1.0 else scale * acc)
    if not emit_norm:
        refs[2 * n + 1][...] = y
        return
    g_ref, o_ref, xg_ref, ss_ref = refs[2 * n + 1:]
    o_ref[...] = y
    xg_ref[...] = (y * g_ref[...]).astype(xg_ref.dtype)
    ss_ref[0] = _col_to_row(jnp.sum(y * y, axis=1, keepdims=True))


def _matmul_resid(pieces, w, x, scale, name, gain=None):
    m, n = x.shape
    ktot = w.shape[0]
    tm = _pick(m, (1024, 512, 256, 128)) if ktot <= 4096 else _pick(m, (512, 256, 128))
    tn = _pick(n, (512, 256, 128))
    a_specs, w_specs, row = [], [], 0
    for a in pieces:
        kp = a.shape[1]
        assert row % kp == 0
        a_specs.append(pl.BlockSpec((tm, kp), lambda i, j: (i, 0)))
        w_specs.append(pl.BlockSpec((kp, tn), lambda i, j, rb=row // kp: (rb, j)))
        row += kp
    assert row == ktot
    xo_spec = pl.BlockSpec((tm, tn), lambda i, j: (i, j))
    in_specs, out_specs, args = a_specs + w_specs + [xo_spec], xo_spec, [*pieces, *([w] * len(pieces)), x]
    out_shape = jax.ShapeDtypeStruct((m, n), _F32)
    if gain is not None:
        in_specs.append(pl.BlockSpec((1, tn), lambda i, j: (0, j)))
        args.append(gain.reshape(1, n))
        out_specs = [xo_spec, xo_spec, pl.BlockSpec((1, 1, tm), lambda i, j: (j, 0, i))]
        out_shape = [out_shape, jax.ShapeDtypeStruct((m, n), _BF16), jax.ShapeDtypeStruct((n // tn, 1, m), _F32)]
    return pl.pallas_call(
        functools.partial(_matmul_resid_kernel, n=len(pieces), scale=scale, emit_norm=gain is not None),
        grid=(m // tm, n // tn),
        in_specs=in_specs,
        out_specs=out_specs,
        out_shape=out_shape,
        compiler_params=_params(("parallel", "parallel"), 56),
        name=name,
    )(*args)


def _ffn_up_kernel(h_ref, *refs, scaled):
    ss_ref = refs[0] if scaled else None
    wg_ref, wu_ref, wd_ref, o_ref, wdo_ref, wcat = refs[1:] if scaled else refs
    tn = wg_ref.shape[2]

    @pl.when(pl.program_id(1) == 0)
    def _():
        wcat[:, :tn] = wg_ref[0].astype(wcat.dtype)
        wcat[:, tn:] = wu_ref[0].astype(wcat.dtype)
        wdo_ref[...] = wd_ref[0].astype(wdo_ref.dtype)

    r = jnp.dot(h_ref[...], wcat[...], preferred_element_type=_F32)
    if scaled:
        r = _row_scale(r, _inv_rms(ss_ref, h_ref.shape[1]))
    a, b = r[:, :tn], r[:, tn:]
    o_ref[...] = (a * jax.nn.sigmoid(a) * b).astype(o_ref.dtype)


def _ffn_up(h, stats, w_gate, w_up, w_down, layer, name):
    m, d = h.shape
    f = w_gate.shape[2]
    tm = _pick(m, (1024, 512, 256, 128))
    tn = _pick(f, (256, 128))
    w_spec = pl.BlockSpec((1, d, tn), lambda j, i: (layer, 0, j))
    stat_specs = [] if stats is None else [pl.BlockSpec((stats.shape[0], 1, tm), lambda j, i: (0, 0, i))]
    return pl.pallas_call(
        functools.partial(_ffn_up_kernel, scaled=stats is not None),
        grid=(f // tn, m // tm),
        in_specs=[pl.BlockSpec((tm, d), lambda j, i: (i, 0))] + stat_specs + [
            w_spec, w_spec, pl.BlockSpec((1, tn, d), lambda j, i: (layer, j, 0))],
        out_specs=[pl.BlockSpec((tm, tn), lambda j, i: (i, j)),
                   pl.BlockSpec((tn, d), lambda j, i: (j, 0))],
        out_shape=[jax.ShapeDtypeStruct((m, f), _BF16), jax.ShapeDtypeStruct((f, d), _BF16)],
        scratch_shapes=[pltpu.VMEM((d, 2 * tn), _BF16)],
        compiler_params=_params(("arbitrary", "arbitrary"), 56),
        name=name,
    )(h, *([] if stats is None else [stats]), w_gate, w_up, w_down)


def _ffn(x, h, stats, w_gate, w_up, w_down, layer, name, next_gain):
    hidden, wd = _ffn_up(h, stats, w_gate, w_up, w_down, layer, name + "_up")
    return _matmul_resid([hidden], wd, x, FFN_RESIDUAL, name + "_down", gain=next_gain)


def _t5_bucket_np(rel):
    half = REL_BUCKETS // 2
    max_exact = half // 2
    ret = np.where(rel > 0, half, 0)
    n = np.abs(rel)
    nf = np.maximum(n, 1).astype(np.float64)
    val = np.log(nf / max_exact) / math.log(REL_MAX_DISTANCE / max_exact) * (half - max_exact)
    large = np.minimum(max_exact + val.astype(np.int64), half - 1)
    return (ret + np.where(n < max_exact, n, large)).astype(np.int32)


def _branch_dilations():
    return tuple(sorted((d for _, d in A_BRANCHES), reverse=True))


def _attn_bias_tiles(rel_bias):
    tq, halo = ATT_TQ, ATT_HALO
    win = tq + 2 * halo
    period = 2 * win
    c = np.arange(win)[None, :]
    ok_first = np.broadcast_to(c >= halo, (tq, win))
    ok_last = np.broadcast_to(c < tq + halo, (tq, win))
    branches = []
    for dilation in _branch_dilations():
        rel = np.arange(-halo, halo + 1)
        band = rel_bias[_t5_bucket_np(rel * dilation)].astype(_F32).T
        heads = band.shape[0]
        row = jnp.full((heads, period), NEG_INF, _F32).at[:, :2 * halo + 1].set(band)
        toeplitz = jnp.tile(row, (1, tq))[:, :tq * (period - 1)].reshape(heads, tq, period - 1)[:, :, :win]
        branches.append(jnp.stack(
            [jnp.where(mk[None], toeplitz, NEG_INF)
             for mk in (np.ones((tq, win), bool), ok_first, ok_last, ok_first & ok_last)], axis=1))
    return jnp.stack(branches, axis=1)


def _attn_kernel(bm_ref, q_ref, k_ref, v_ref, o_ref, wide, qd, kd, vd, acc, mrun, lrun, *, scale, dilations):
    seq = q_ref.shape[1]
    tq, halo = ATT_TQ, ATT_HALO
    win = tq + 2 * halo
    ntiles = seq // tq
    zeros = jnp.zeros((halo, HEAD_DIM), kd.dtype)
    ones = jnp.ones((win, LANES), vd.dtype)

    for g, d in enumerate(dilations):
        sub = seq // d
        nt = sub // tq
        chunk = wide.shape[0]
        per = chunk // d
        for src, dst, padded in ((q_ref, qd, False), (k_ref, kd, True), (v_ref, vd, True)):
            if d == 1 and not padded:
                continue
            stride_r = sub + 2 * halo if padded else sub
            first = halo if padded else 0
            for c in range(seq // chunk):
                def move(i, carry, c=c, src=src, dst=dst, first=first, widen=d > 1):
                    off = pl.multiple_of(i * ATT_OUT_ROWS, ATT_OUT_ROWS)
                    rows = src[0, pl.ds(c * chunk + off, ATT_OUT_ROWS), :]
                    if widen:
                        wide[pl.ds(off, ATT_OUT_ROWS), :] = rows.astype(_F32)
                    else:
                        dst[pl.ds(pl.multiple_of(first + c * chunk + off, ATT_HALO), ATT_OUT_ROWS), :] = rows
                    return carry

                lax.fori_loop(0, chunk // ATT_OUT_ROWS, move, 0)
                if d == 1:
                    continue
                for r in range(d):
                    base = r * stride_r + first + c * per
                    dst[base:base + per, :] = wide[pl.ds(r, per, stride=d), :].astype(dst.dtype)
            if padded:
                for r in range(d):
                    dst[r * stride_r:r * stride_r + halo, :] = zeros
                    dst[r * stride_r + halo + sub:(r + 1) * stride_r, :] = zeros

        def tile(tile_idx, g=g, d=d, nt=nt):
            r = tile_idx // nt
            t = tile_idx - r * nt
            q0 = pl.multiple_of(tile_idx * tq, tq)
            k0 = pl.multiple_of(tile_idx * tq + r * (2 * halo), 2 * halo)
            q = q_ref[0, pl.ds(q0, tq), :] if d == 1 else qd[pl.ds(q0, tq), :]
            kw = kd[pl.ds(k0, win), :]
            vw = vd[pl.ds(k0, win), :]
            s = lax.dot_general(q, kw, (((1,), (1,)), ((), ())), preferred_element_type=_F32)
            variant = jnp.where(t == 0, 1, 0) + jnp.where(t == nt - 1, 2, 0)
            s = s * scale + bm_ref[0, g, variant]
            mx = jnp.max(s, axis=-1, keepdims=True)
            p = jnp.exp(s - mx)
            pv = jnp.dot(p.astype(vw.dtype), jnp.concatenate([vw, ones], axis=1), preferred_element_type=_F32)
            pv, den = pv[:, :HEAD_DIM], pv[:, HEAD_DIM:]
            idx = pl.ds(q0, tq) if d == 1 else pl.ds(t * (tq * d) + r, tq, stride=d)
            if g == 0:
                acc[idx, :] = pv
                mrun[idx, :] = jnp.broadcast_to(mx, (tq, LANES))
                lrun[idx, :] = den
            else:
                m_old = mrun[idx, :]
                m_new = jnp.maximum(m_old, mx)
                c_old = jnp.exp(m_old - m_new)
                c_new = jnp.exp(mx - m_new)
                acc[idx, :] = acc[idx, :] * c_old + pv * c_new
                lrun[idx, :] = lrun[idx, :] * c_old + den * c_new
                mrun[idx, :] = m_new

        def tiles(it, carry, tile=tile):
            for u in range(ATT_UNROLL):
                tile(it * ATT_UNROLL + u)
            return carry

        lax.fori_loop(0, ntiles // ATT_UNROLL, tiles, 0)

    def finish(it, carry):
        rows = pl.ds(pl.multiple_of(it * ATT_OUT_ROWS, ATT_OUT_ROWS), ATT_OUT_ROWS)
        o_ref[0, rows, :] = (acc[rows, :] / lrun[rows, :]).astype(o_ref.dtype)
        return carry

    lax.fori_loop(0, seq // ATT_OUT_ROWS, finish, 0)


def _dilated_mixture(qkv, rel_bias):
    b, s, _ = qkv.shape
    h, e = HEADS_A, HEAD_DIM
    dilations = _branch_dilations()
    for window, d in A_BRANCHES:
        assert window // (2 * d) == ATT_HALO and (s // d) % ATT_TQ == 0
    assert (s // ATT_TQ) % ATT_UNROLL == 0 and s % ATT_OUT_ROWS == 0
    chunk = min(ATT_CHUNK, s)
    assert s % chunk == 0 and chunk % ATT_OUT_ROWS == 0 and all((chunk // d) % 16 == 0 for d in dilations)
    win = ATT_TQ + 2 * ATT_HALO
    pad_rows = s + 2 * ATT_HALO * max(dilations)

    def in_spec(which):
        return pl.BlockSpec((1, s, e), lambda hi, bi: (bi, 0, which * h + hi))

    return pl.pallas_call(
        functools.partial(_attn_kernel, scale=1.0 / math.sqrt(e), dilations=dilations),
        grid=(h, b),
        in_specs=[pl.BlockSpec((1, len(dilations), 4, ATT_TQ, win), lambda hi, bi: (hi, 0, 0, 0, 0)),
                  in_spec(0), in_spec(1), in_spec(2)],
        out_specs=pl.BlockSpec((1, s, e), lambda hi, bi: (bi, 0, hi)),
        out_shape=jax.ShapeDtypeStruct((b, s, h * e), _BF16),
        scratch_shapes=[pltpu.VMEM((chunk, e), _F32), pltpu.VMEM((s, e), _BF16),
                        pltpu.VMEM((pad_rows, e), _BF16), pltpu.VMEM((pad_rows, e), _BF16),
                        pltpu.VMEM((s, e), _F32), pltpu.VMEM((s, LANES), _F32), pltpu.VMEM((s, LANES), _F32)],
        compiler_params=_params(("parallel", "parallel"), 56),
        name="dilated_attention",
    )(_attn_bias_tiles(rel_bias), qkv, qkv, qkv)


_HI = lax.Precision.HIGHEST


def _fft_stage1_kernel(f_ref, u_ref, ar_ref, ai_ref):
    n1 = u_ref.shape[1]
    r = jnp.dot(f_ref[...], u_ref[0], precision=_HI, preferred_element_type=_F32)
    ar_ref[0] = r[:n1]
    ai_ref[0] = r[n1:]


def _fft_stage2_kernel(twc_ref, tws_ref, c2_ref, s2_ref, cc_ref, sc_ref, ar_ref, ai_ref, o_ref):
    dot = functools.partial(jnp.dot, precision=_HI, preferred_element_type=_F32)
    ar, ai = ar_ref[0], ai_ref[0]
    c, s = twc_ref[...], tws_ref[...]
    pr = ar * c + ai * s
    pi = ai * c - ar * s
    c2, s2 = c2_ref[...], s2_ref[...]
    zr = dot(c2, pr) + dot(s2, pi)
    zi = dot(c2, pi) - dot(s2, pr)
    gd = cc_ref.shape[0]
    for g in range(ar.shape[1] // gd):
        cols = slice(g * gd, (g + 1) * gd)
        f = dot(zr[:, cols], cc_ref[...]) + dot(zi[:, cols], sc_ref[...])
        o_ref[0, :, cols] = f.astype(o_ref.dtype)


def _fourier_mix(u):
    b, s, width = u.shape
    gd = FNET_GROUP_DIM
    n2 = 128
    n1 = s // n2
    assert n1 * n2 == s and n1 % 8 == 0

    def trig(n, rows, cols):
        ang = 2.0 * np.pi * ((np.arange(rows)[:, None] * np.arange(cols)[None, :]) % n) / n
        return np.cos(ang), np.sin(ang)

    c1, s1 = trig(n1, n1, n1)
    f1 = jnp.asarray(np.concatenate([c1, -s1], axis=0), _F32)
    tc, tsn = trig(s, n1, n2)
    twc = jnp.asarray(tc.reshape(s, 1), _F32)
    tws = jnp.asarray(tsn.reshape(s, 1), _F32)
    c2, s2 = trig(n2, n2, n2)
    norm = 1.0 / math.sqrt(s * gd)
    cc, sc = trig(gd, gd, gd)
    c2, s2 = jnp.asarray(c2, _F32), jnp.asarray(s2, _F32)
    cc, sc = jnp.asarray(cc * norm, _F32), jnp.asarray(sc * norm, _F32)

    lanes = n2 * width
    tl = _pick(lanes, (4096, 2048, 1024, 512, 256, 128))
    a_spec = pl.BlockSpec((1, n1, tl), lambda bi, j: (bi, 0, j))
    ar, ai = pl.pallas_call(
        _fft_stage1_kernel,
        grid=(b, lanes // tl),
        in_specs=[pl.BlockSpec((2 * n1, n1), lambda bi, j: (0, 0)), a_spec],
        out_specs=[a_spec, a_spec],
        out_shape=[jax.ShapeDtypeStruct((b, n1, lanes), _F32)] * 2,
        compiler_params=_params(("parallel", "parallel"), 32),
        name="fft_stage1",
    )(f1, u.reshape(b, n1, lanes))

    tw_spec = pl.BlockSpec((n2, 1), lambda bi, k1: (k1, 0))
    m_spec = pl.BlockSpec((n2, n2), lambda bi, k1: (0, 0))
    g_spec = pl.BlockSpec((gd, gd), lambda bi, k1: (0, 0))
    z_spec = pl.BlockSpec((1, n2, width), lambda bi, k1: (bi, k1, 0))
    out = pl.pallas_call(
        _fft_stage2_kernel,
        grid=(b, n1),
        in_specs=[tw_spec, tw_spec, m_spec, m_spec, g_spec, g_spec, z_spec, z_spec],
        out_specs=pl.BlockSpec((1, n2, width), lambda bi, k1: (bi, 0, k1)),
        out_shape=jax.ShapeDtypeStruct((b, n2, n1 * width), _BF16),
        compiler_params=_params(("parallel", "parallel"), 32),
        name="fft_stage2",
    )(twc, tws, c2, s2, cc, sc, ar.reshape(b, s, width), ai.reshape(b, s, width))
    return out.reshape(b, s, width)


CONV_HALO = 16
CONV_ROWS = 64


def _glu_conv_kernel(w_ref, b_ref, vc_ref, vp_ref, vn_ref, gc_ref, gp_ref, gn_ref, y_ref, ext, shifted, *, width):
    i, last = pl.program_id(1), pl.num_programs(1) - 1
    ts, tc = vc_ref.shape[1], vc_ref.shape[2]
    halo, pad = CONV_HALO, (width - 1) // 2

    def glu(v_ref, g_ref):
        return v_ref[0] * jax.nn.sigmoid(g_ref[0])

    ext[0:halo, :] = jnp.where(i > 0, glu(vp_ref, gp_ref), 0.0)
    ext[halo:halo + ts, :] = glu(vc_ref, gc_ref)
    ext[halo + ts:, :] = jnp.where(i < last, glu(vn_ref, gn_ref), 0.0)
    rows = min(CONV_ROWS, ts)
    span = shifted.shape[1]
    for cb in range(tc // LANES):
        cols = slice(cb * LANES, (cb + 1) * LANES)
        for k in range(1, SUBLANES):
            shifted[k - 1, :, cols] = ext[k:k + span, cols]
        for r0 in range(0, ts, rows):
            acc = jnp.broadcast_to(b_ref[:, cols], (rows, LANES))
            for t in range(width):
                start = r0 + halo - pad + t
                k, base = start % SUBLANES, start - start % SUBLANES
                tap = ext[base:base + rows, cols] if k == 0 else shifted[k - 1, base:base + rows, cols]
                acc = acc + w_ref[t:t + 1, cols] * tap
            y_ref[0, r0:r0 + rows, cols] = acc


def _glu_conv(proj, conv_w, conv_b):
    b, s, _ = proj.shape
    width, cdim = conv_w.shape
    assert (width - 1) // 2 <= CONV_HALO
    ts = _pick(s, (256, 128))
    tc = _pick(cdim, (512, 256, 128))
    ncb, hb = cdim // tc, ts // CONV_HALO
    nhalo = s // CONV_HALO
    w_pad = jnp.pad(conv_w, ((0, -width % 8), (0, 0)))

    def cur(off):
        return pl.BlockSpec((1, ts, tc), lambda bi, i, c: (bi, i, off * ncb + c))

    def prev(off):
        return pl.BlockSpec((1, CONV_HALO, tc), lambda bi, i, c: (bi, jnp.maximum(i * hb - 1, 0), off * ncb + c))

    def nxt(off):
        return pl.BlockSpec((1, CONV_HALO, tc),
                            lambda bi, i, c: (bi, jnp.minimum((i + 1) * hb, nhalo - 1), off * ncb + c))

    return pl.pallas_call(
        functools.partial(_glu_conv_kernel, width=width),
        grid=(b, s // ts, ncb),
        in_specs=[pl.BlockSpec((w_pad.shape[0], tc), lambda bi, i, c: (0, c)),
                  pl.BlockSpec((1, tc), lambda bi, i, c: (0, c)),
                  cur(0), prev(0), nxt(0), cur(1), prev(1), nxt(1)],
        out_specs=pl.BlockSpec((1, ts, tc), lambda bi, i, c: (bi, i, c)),
        out_shape=jax.ShapeDtypeStruct((b, s, cdim), _F32),
        scratch_shapes=[pltpu.VMEM((ts + 2 * CONV_HALO, tc), _F32),
                        pltpu.VMEM((SUBLANES - 1, ts + 2 * CONV_HALO - SUBLANES, tc), _F32)],
        compiler_params=_params(("parallel", "parallel", "parallel"), 32),
        name="glu_conv",
    )(w_pad, conv_b.reshape(1, cdim), proj, proj, proj, proj, proj, proj)


SHORT_HALO = 8


def _norm_gate_kernel(y_ref, lg_ref, lb_ref, w_ref, db_ref, cc_ref, cp_ref, cn_ref, hc_ref, hp_ref, hn_ref,
                      o_ref, ext, *, width):
    i, last = pl.program_id(1), pl.num_programs(1) - 1
    ts, cdim = y_ref.shape[1], y_ref.shape[2]
    ddim = db_ref.shape[2]
    halo, pad = SHORT_HALO, (width - 1) // 2
    rows = min(CONV_ROWS, ts)

    for r0 in range(0, ts, rows):
        y = y_ref[0, r0:r0 + rows, :]
        yc = y - jnp.mean(y, axis=-1, keepdims=True)
        inv = lax.rsqrt(jnp.mean(yc * yc, axis=-1, keepdims=True) + NORM_EPS)
        n = yc * inv * lg_ref[...] + lb_ref[...]
        o_ref[0, r0:r0 + rows, 0:cdim] = (n * jax.nn.sigmoid(n)).astype(o_ref.dtype)

    ext[0:halo, :] = jnp.where(i > 0, cp_ref[0] * hp_ref[0], 0.0)
    ext[halo:halo + ts, :] = cc_ref[0] * hc_ref[0]
    ext[halo + ts:, :] = jnp.where(i < last, cn_ref[0] * hn_ref[0], 0.0)
    for cb in range(ddim // LANES):
        cols = slice(cb * LANES, (cb + 1) * LANES)
        for r0 in range(0, ts, rows):
            acc = jnp.zeros((rows, LANES), _F32)
            for t in range(width):
                start = r0 + halo - pad + t
                acc = acc + w_ref[t:t + 1, cols] * ext[start:start + rows, cols]
            o_ref[0, r0:r0 + rows, cdim + cb * LANES:cdim + (cb + 1) * LANES] = (
                db_ref[0, r0:r0 + rows, cols] * acc).astype(o_ref.dtype)


def _norm_gate(y, proj, ln_g, ln_b, conv_d_w):
    b, s, cdim = y.shape
    width, ddim = conv_d_w.shape
    assert (width - 1) // 2 <= SHORT_HALO and cdim % ddim == 0
    ts = _pick(s, (256, 128))
    hb, nhalo = ts // SHORT_HALO, s // SHORT_HALO
    base = 2 * cdim // ddim
    w_pad = jnp.pad(conv_d_w, ((0, -width % 8), (0, 0)))

    def cur(off):
        return pl.BlockSpec((1, ts, ddim), lambda bi, i: (bi, i, base + off))

    def prev(off):
        return pl.BlockSpec((1, SHORT_HALO, ddim), lambda bi, i: (bi, jnp.maximum(i * hb - 1, 0), base + off))

    def nxt(off):
        return pl.BlockSpec((1, SHORT_HALO, ddim),
                            lambda bi, i: (bi, jnp.minimum((i + 1) * hb, nhalo - 1), base + off))

    vec = pl.BlockSpec((1, cdim), lambda bi, i: (0, 0))
    return pl.pallas_call(
        functools.partial(_norm_gate_kernel, width=width),
        grid=(b, s // ts),
        in_specs=[pl.BlockSpec((1, ts, cdim), lambda bi, i: (bi, i, 0)), vec, vec,
                  pl.BlockSpec((w_pad.shape[0], ddim), lambda bi, i: (0, 0)),
                  cur(0), cur(1), prev(1), nxt(1), cur(2), prev(2), nxt(2)],
        out_specs=pl.BlockSpec((1, ts, cdim + ddim), lambda bi, i: (bi, i, 0)),
        out_shape=jax.ShapeDtypeStruct((b, s, cdim + ddim), _BF16),
        scratch_shapes=[pltpu.VMEM((ts + 2 * SHORT_HALO, ddim), _F32)],
        compiler_params=_params(("parallel", "parallel"), 48),
        name="norm_gate",
    )(y, ln_g.reshape(1, cdim), ln_b.reshape(1, cdim), w_pad, proj, proj, proj, proj, proj, proj, proj)


def _mixer_ab(x, h, stats, w_in, w_out, index, rel_bias, b, s, next_gain):
    qkv_dim = HEADS_A * HEAD_DIM
    qkv = _scaled_matmul(h, stats, w_in, index, 0, 3 * qkv_dim, _BF16, "in_ab_qkv")
    u = _scaled_matmul(h, stats, w_in, index, 3 * qkv_dim, w_in.shape[2] - 3 * qkv_dim, _F32, "in_ab_u")
    a_out = _dilated_mixture(qkv.reshape(b, s, 3 * qkv_dim), rel_bias).reshape(b * s, qkv_dim)
    b_out = _fourier_mix(u.reshape(b, s, -1)).reshape(b * s, -1)
    return _matmul_resid([a_out, b_out], w_out[index].astype(_BF16), x, 1.0, "out_ab", gain=next_gain)


def _mixer_cd(x, h, stats, w_in, conv_c_w, conv_c_b, ln_c_g, ln_c_b, conv_d_w, w_out, index, b, s, next_gain):
    proj = _scaled_matmul(h, stats, w_in, index, 0, w_in.shape[2], _F32, "in_cd").reshape(b, s, -1)
    y = _glu_conv(proj, conv_c_w[index], conv_c_b[index])
    mixed = _norm_gate(y, proj, ln_c_g[index], ln_c_b[index], conv_d_w[index]).reshape(b * s, -1)
    return _matmul_resid([mixed], w_out[index].astype(_BF16), x, 1.0, "out_cd", gain=next_gain)


def kernel(x, ffn1_norm, ffn1_w_gate, ffn1_w_up, ffn1_w_down, mix_norm, ffn2_norm, ffn2_w_gate, ffn2_w_up,
           ffn2_w_down, rel_bias, w_in_ab, w_out_ab, w_in_cd, conv_c_w, conv_c_b, ln_c_g, ln_c_b, conv_d_w,
           w_out_cd, final_norm):
    b, s, d = x.shape
    depth = ffn1_norm.shape[0]
    x = x.reshape(b * s, d)
    h, stats = _rmsnorm(x, ffn1_norm[0], _BF16, "ffn1_l0_norm"), None
    for layer in range(depth):
        x, h, stats = _ffn(x, h, stats, ffn1_w_gate, ffn1_w_up, ffn1_w_down, layer, f"ffn1_l{layer}",
                           mix_norm[layer])
        i = layer // 2
        if layer % 2 == 0:
            x, h, stats = _mixer_ab(x, h, stats, w_in_ab, w_out_ab, i, rel_bias, b, s, ffn2_norm[layer])
        else:
            x, h, stats = _mixer_cd(x, h, stats, w_in_cd, conv_c_w, conv_c_b, ln_c_g, ln_c_b, conv_d_w,
                                    w_out_cd, i, b, s, ffn2_norm[layer])
        if layer + 1 < depth:
            x, h, stats = _ffn(x, h, stats, ffn2_w_gate, ffn2_w_up, ffn2_w_down, layer, f"ffn2_l{layer}",
                               ffn1_norm[layer + 1])
        else:
            x = _ffn(x, h, stats, ffn2_w_gate, ffn2_w_up, ffn2_w_down, layer, f"ffn2_l{layer}", None)
    return _rmsnorm(x, final_norm, _F32, "final_norm").reshape(b, s, d)
```

```python
import functools
import math

import jax
import jax.numpy as jnp
import numpy as np
from jax import lax
from jax.experimental import pallas as pl
from jax.experimental.pallas import tpu as pltpu

HEAD_DIM = 128
HEADS_A = 24
A_BRANCHES = ((128, 1), (512, 4), (2048, 16))
FNET_GROUPS = 8
FNET_GROUP_DIM = 128
REL_BUCKETS = 32
REL_MAX_DISTANCE = 1024
CONV_C_DIM = 2048
CONV_D_DIM = 2048
FFN_RESIDUAL = 0.5
NORM_EPS = 1e-6
NEG_INF = -1e30

LANES = 128
SUBLANES = 8
ATT_TQ = 128
ATT_HALO = 64
ATT_UNROLL = 16
ATT_OUT_ROWS = 256
ATT_CHUNK = 2048
MIB = 1024 * 1024

_F32 = jnp.float32
_BF16 = jnp.bfloat16


def _params(sem, vmem_mib):
    return pltpu.CompilerParams(dimension_semantics=sem, vmem_limit_bytes=vmem_mib * MIB)


def _pick(n, prefs):
    for p in prefs:
        if n % p == 0:
            return p
    return n


def _rms(x, g):
    inv = lax.rsqrt(jnp.mean(x * x, axis=-1, keepdims=True) + NORM_EPS)
    return x * inv * g


def _col_to_row(col):
    parts = [jnp.transpose(jnp.broadcast_to(col[k:k + LANES], (LANES, LANES)))[0:1, :]
             for k in range(0, col.shape[0], LANES)]
    return parts[0] if len(parts) == 1 else jnp.concatenate(parts, axis=1)


def _inv_rms(ss_ref, d):
    tot = ss_ref[0]
    for k in range(1, ss_ref.shape[0]):
        tot = tot + ss_ref[k]
    return lax.rsqrt(tot / d + NORM_EPS)


def _row_scale(r, inv_row):
    reps = r.shape[1] // LANES
    blocks = []
    for k in range(0, r.shape[0], LANES):
        bc = jnp.transpose(jnp.broadcast_to(inv_row[:, k:k + LANES], (LANES, LANES)))
        blocks.append(r[k:k + LANES] * (bc if reps == 1 else jnp.concatenate([bc] * reps, axis=1)))
    return blocks[0] if len(blocks) == 1 else jnp.concatenate(blocks, axis=0)


def _rmsnorm_kernel(x_ref, g_ref, o_ref):
    o_ref[...] = _rms(x_ref[...], g_ref[...]).astype(o_ref.dtype)


def _rmsnorm(x, g, out_dtype, name):
    m, d = x.shape
    tm = _pick(m, (256, 128))
    return pl.pallas_call(
        _rmsnorm_kernel,
        grid=(m // tm,),
        in_specs=[pl.BlockSpec((tm, d), lambda i: (i, 0)), pl.BlockSpec((1, d), lambda i: (0, 0))],
        out_specs=pl.BlockSpec((tm, d), lambda i: (i, 0)),
        out_shape=jax.ShapeDtypeStruct((m, d), out_dtype),
        compiler_params=_params(("parallel",), 32),
        name=name,
    )(x, g.reshape(1, d))


def _scaled_matmul_kernel(h_ref, ss_ref, w_ref, o_ref, wbf):
    @pl.when(pl.program_id(1) == 0)
    def _():
        wbf[...] = w_ref[0].astype(wbf.dtype)

    r = jnp.dot(h_ref[...], wbf[...], preferred_element_type=_F32)
    o_ref[...] = _row_scale(r, _inv_rms(ss_ref, h_ref.shape[1])).astype(o_ref.dtype)


def _scaled_matmul(h, stats, w, index, col0, width, out_dtype, name):
    m, d = h.shape
    tm = _pick(m, (1024, 512, 256, 128))
    tn = _pick(math.gcd(col0, width), (512, 256, 128))
    return pl.pallas_call(
        _scaled_matmul_kernel,
        grid=(width // tn, m // tm),
        in_specs=[pl.BlockSpec((tm, d), lambda j, i: (i, 0)),
                  pl.BlockSpec((stats.shape[0], 1, tm), lambda j, i: (0, 0, i)),
                  pl.BlockSpec((1, d, tn), lambda j, i: (index, 0, col0 // tn + j))],
        out_specs=pl.BlockSpec((tm, tn), lambda j, i: (i, j)),
        out_shape=jax.ShapeDtypeStruct((m, width), out_dtype),
        scratch_shapes=[pltpu.VMEM((d, tn), _BF16)],
        compiler_params=_params(("arbitrary", "arbitrary"), 56),
        name=name,
    )(h, stats, w)


def _matmul_resid_kernel(*refs, n, scale, emit_norm):
    a_refs, w_refs, x_ref = refs[:n], refs[n:2 * n], refs[2 * n]
    acc = jnp.dot(a_refs[0][...], w_refs[0][...], preferred_element_type=_F32)
    for a_ref, w_ref in zip(a_refs[1:], w_refs[1:]):
        acc = acc + jnp.dot(a_ref[...], w_ref[...], preferred_element_type=_F32)
    y = x_ref[...] + (acc if scale == 1.0 else scale * acc)
    if not emit_norm:
        refs[2 * n + 1][...] = y
        return
    g_ref, o_ref, xg_ref, ss_ref = refs[2 * n + 1:]
    o_ref[...] = y
    xg_ref[...] = (y * g_ref[...]).astype(xg_ref.dtype)
    ss_ref[0] = _col_to_row(jnp.sum(y * y, axis=1, keepdims=True))


def _matmul_resid(pieces, w, x, scale, name, gain=None):
    m, n = x.shape
    ktot = w.shape[0]
    tm = _pick(m, (1024, 512, 256, 128)) if ktot <= 4096 else _pick(m, (512, 256, 128))
    tn = _pick(n, (512, 256, 128))
    a_specs, w_specs, row = [], [], 0
    for a in pieces:
        kp = a.shape[1]
        assert row % kp == 0
        a_specs.append(pl.BlockSpec((tm, kp), lambda i, j: (i, 0)))
        w_specs.append(pl.BlockSpec((kp, tn), lambda i, j, rb=row // kp: (rb, j)))
        row += kp
    assert row == ktot
    xo_spec = pl.BlockSpec((tm, tn), lambda i, j: (i, j))
    in_specs, out_specs, args = a_specs + w_specs + [xo_spec], xo_spec, [*pieces, *([w] * len(pieces)), x]
    out_shape = jax.ShapeDtypeStruct((m, n), _F32)
    if gain is not None:
        in_specs.append(pl.BlockSpec((1, tn), lambda i, j: (0, j)))
        args.append(gain.reshape(1, n))
        out_specs = [xo_spec, xo_spec, pl.BlockSpec((1, 1, tm), lambda i, j: (j, 0, i))]
        out_shape = [out_shape, jax.ShapeDtypeStruct((m, n), _BF16), jax.ShapeDtypeStruct((n // tn, 1, m), _F32)]
    return pl.pallas_call(
        functools.partial(_matmul_resid_kernel, n=len(pieces), scale=scale, emit_norm=gain is not None),
        grid=(m // tm, n // tn),
        in_specs=in_specs,
        out_specs=out_specs,
        out_shape=out_shape,
        compiler_params=_params(("parallel", "parallel"), 56),
        name=name,
    )(*args)


def _ffn_up_kernel(h_ref, *refs, scaled):
    ss_ref = refs[0] if scaled else None
    wg_ref, wu_ref, wd_ref, o_ref, wdo_ref, wcat = refs[1:] if scaled else refs
    tn = wg_ref.shape[2]

    @pl.when(pl.program_id(1) == 0)
    def _():
        wcat[:, :tn] = wg_ref[0].astype(wcat.dtype)
        wcat[:, tn:] = wu_ref[0].astype(wcat.dtype)
        wdo_ref[...] = wd_ref[0].astype(wdo_ref.dtype)

    r = jnp.dot(h_ref[...], wcat[...], preferred_element_type=_F32)
    if scaled:
        r = _row_scale(r, _inv_rms(ss_ref, h_ref.shape[1]))
    a, b = r[:, :tn], r[:, tn:]
    o_ref[...] = (a * jax.nn.sigmoid(a) * b).astype(o_ref.dtype)


def _ffn_up(h, stats, w_gate, w_up, w_down, layer, name):
    m, d = h.shape
    f = w_gate.shape[2]
    tm = _pick(m, (1024, 512, 256, 128))
    tn = _pick(f, (256, 128))
    w_spec = pl.BlockSpec((1, d, tn), lambda j, i: (layer, 0, j))
    stat_specs = [] if stats is None else [pl.BlockSpec((stats.shape[0], 1, tm), lambda j, i: (0, 0, i))]
    return pl.pallas_call(
        functools.partial(_ffn_up_kernel, scaled=stats is not None),
        grid=(f // tn, m // tm),
        in_specs=[pl.BlockSpec((tm, d), lambda j, i: (i, 0))] + stat_specs + [
            w_spec, w_spec, pl.BlockSpec((1, tn, d), lambda j, i: (layer, j, 0))],
        out_specs=[pl.BlockSpec((tm, tn), lambda j, i: (i, j)),
                   pl.BlockSpec((tn, d), lambda j, i: (j, 0))],
        out_shape=[jax.ShapeDtypeStruct((m, f), _BF16), jax.ShapeDtypeStruct((f, d), _BF16)],
        scratch_shapes=[pltpu.VMEM((d, 2 * tn), _BF16)],
        compiler_params=_params(("arbitrary", "arbitrary"), 56),
        name=name,
    )(h, *([] if stats is None else [stats]), w_gate, w_up, w_down)


def _ffn(x, h, stats, w_gate, w_up, w_down, layer, name, next_gain):
    hidden, wd = _ffn_up(h, stats, w_gate, w_up, w_down, layer, name + "_up")
    return _matmul_resid([hidden], wd, x, FFN_RESIDUAL, name + "_down", gain=next_gain)


def _t5_bucket_np(rel):
    half = REL_BUCKETS // 2
    max_exact = half // 2
    ret = np.where(rel > 0, half, 0)
    n = np.abs(rel)
    nf = np.maximum(n, 1).astype(np.float64)
    val = np.log(nf / max_exact) / math.log(REL_MAX_DISTANCE / max_exact) * (half - max_exact)
    large = np.minimum(max_exact + val.astype(np.int64), half - 1)
    return (ret + np.where(n < max_exact, n, large)).astype(np.int32)


def _branch_dilations():
    return tuple(sorted((d for _, d in A_BRANCHES), reverse=True))


def _attn_bias_tiles(rel_bias):
    tq, halo = ATT_TQ, ATT_HALO
    win = tq + 2 * halo
    period = 2 * win
    c = np.arange(win)[None, :]
    ok_first = np.broadcast_to(c >= halo, (tq, win))
    ok_last = np.broadcast_to(c < tq + halo, (tq, win))
    branches = []
    for dilation in _branch_dilations():
        rel = np.arange(-halo, halo + 1)
        band = rel_bias[_t5_bucket_np(rel * dilation)].astype(_F32).T
        heads = band.shape[0]
        row = jnp.full((heads, period), NEG_INF, _F32).at[:, :2 * halo + 1].set(band)
        toeplitz = jnp.tile(row, (1, tq))[:, :tq * (period - 1)].reshape(heads, tq, period - 1)[:, :, :win]
        branches.append(jnp.stack(
            [jnp.where(mk[None], toeplitz, NEG_INF)
             for mk in (np.ones((tq, win), bool), ok_first, ok_last, ok_first & ok_last)], axis=1))
    return jnp.stack(branches, axis=1)


def _attn_kernel(bm_ref, q_ref, k_ref, v_ref, o_ref, wide, qd, kd, vd, acc, mrun, lrun, *, scale, dilations):
    seq = q_ref.shape[1]
    tq, halo = ATT_TQ, ATT_HALO
    win = tq + 2 * halo
    ntiles = seq // tq
    zeros = jnp.zeros((halo, HEAD_DIM), kd.dtype)
    ones = jnp.ones((win, LANES), vd.dtype)

    for g, d in enumerate(dilations):
        sub = seq // d
        nt = sub // tq
        chunk = wide.shape[0]
        per = chunk // d
        for src, dst, padded in ((q_ref, qd, False), (k_ref, kd, True), (v_ref, vd, True)):
            if d == 1 and not padded:
                continue
            stride_r = sub + 2 * halo if padded else sub
            first = halo if padded else 0
            for c in range(seq // chunk):
                def move(i, carry, c=c, src=src, dst=dst, first=first, widen=d > 1):
                    off = pl.multiple_of(i * ATT_OUT_ROWS, ATT_OUT_ROWS)
                    rows = src[0, pl.ds(c * chunk + off, ATT_OUT_ROWS), :]
                    if widen:
                        wide[pl.ds(off, ATT_OUT_ROWS), :] = rows.astype(_F32)
                    else:
                        dst[pl.ds(pl.multiple_of(first + c * chunk + off, ATT_HALO), ATT_OUT_ROWS), :] = rows
                    return carry

                lax.fori_loop(0, chunk // ATT_OUT_ROWS, move, 0)
                if d == 1:
                    continue
                for r in range(d):
                    base = r * stride_r + first + c * per
                    dst[base:base + per, :] = wide[pl.ds(r, per, stride=d), :].astype(dst.dtype)
            if padded:
                for r in range(d):
                    dst[r * stride_r:r * stride_r + halo, :] = zeros
                    dst[r * stride_r + halo + sub:(r + 1) * stride_r, :] = zeros

        def tile(tile_idx, g=g, d=d, nt=nt):
            r = tile_idx // nt
            t = tile_idx - r * nt
            q0 = pl.multiple_of(tile_idx * tq, tq)
            k0 = pl.multiple_of(tile_idx * tq + r * (2 * halo), 2 * halo)
            q = q_ref[0, pl.ds(q0, tq), :] if d == 1 else qd[pl.ds(q0, tq), :]
            kw = kd[pl.ds(k0, win), :]
            vw = vd[pl.ds(k0, win), :]
            s = lax.dot_general(q, kw, (((1,), (1,)), ((), ())), preferred_element_type=_F32)
            variant = jnp.where(t == 0, 1, 0) + jnp.where(t == nt - 1, 2, 0)
            s = s * scale + bm_ref[0, g, variant]
            mx = jnp.max(s, axis=-1, keepdims=True)
            p = jnp.exp(s - mx)
            pv = jnp.dot(p.astype(vw.dtype), jnp.concatenate([vw, ones], axis=1), preferred_element_type=_F32)
            pv, den = pv[:, :HEAD_DIM], pv[:, HEAD_DIM:]
            idx = pl.ds(q0, tq) if d == 1 else pl.ds(t * (tq * d) + r, tq, stride=d)
            if g == 0:
                acc[idx, :] = pv
                mrun[idx, :] = jnp.broadcast_to(mx, (tq, LANES))
                lrun[idx, :] = den
            else:
                m_old = mrun[idx, :]
                m_new = jnp.maximum(m_old, mx)
                c_old = jnp.exp(m_old - m_new)
                c_new = jnp.exp(mx - m_new)
                acc[idx, :] = acc[idx, :] * c_old + pv * c_new
                lrun[idx, :] = lrun[idx, :] * c_old + den * c_new
                mrun[idx, :] = m_new

        def tiles(it, carry, tile=tile):
            for u in range(ATT_UNROLL):
                tile(it * ATT_UNROLL + u)
            return carry

        lax.fori_loop(0, ntiles // ATT_UNROLL, tiles, 0)

    def finish(it, carry):
        rows = pl.ds(pl.multiple_of(it * ATT_OUT_ROWS, ATT_OUT_ROWS), ATT_OUT_ROWS)
        o_ref[0, rows, :] = (acc[rows, :] / lrun[rows, :]).astype(o_ref.dtype)
        return carry

    lax.fori_loop(0, seq // ATT_OUT_ROWS, finish, 0)


def _dilated_mixture(qkv, rel_bias):
    b, s, _ = qkv.shape
    h, e = HEADS_A, HEAD_DIM
    dilations = _branch_dilations()
    for window, d in A_BRANCHES:
        assert window // (2 * d) == ATT_HALO and (s // d) % ATT_TQ == 0
    assert (s // ATT_TQ) % ATT_UNROLL == 0 and s % ATT_OUT_ROWS == 0
    chunk = min(ATT_CHUNK, s)
    assert s % chunk == 0 and chunk % ATT_OUT_ROWS == 0 and all((chunk // d) % 16 == 0 for d in dilations)
    win = ATT_TQ + 2 * ATT_HALO
    pad_rows = s + 2 * ATT_HALO * max(dilations)

    def in_spec(which):
        return pl.BlockSpec((1, s, e), lambda hi, bi: (bi, 0, which * h + hi))

    return pl.pallas_call(
        functools.partial(_attn_kernel, scale=1.0 / math.sqrt(e), dilations=dilations),
        grid=(h, b),
        in_specs=[pl.BlockSpec((1, len(dilations), 4, ATT_TQ, win), lambda hi, bi: (hi, 0, 0, 0, 0)),
                  in_spec(0), in_spec(1), in_spec(2)],
        out_specs=pl.BlockSpec((1, s, e), lambda hi, bi: (bi, 0, hi)),
        out_shape=jax.ShapeDtypeStruct((b, s, h * e), _BF16),
        scratch_shapes=[pltpu.VMEM((chunk, e), _F32), pltpu.VMEM((s, e), _BF16),
                        pltpu.VMEM((pad_rows, e), _BF16), pltpu.VMEM((pad_rows, e), _BF16),
                        pltpu.VMEM((s, e), _F32), pltpu.VMEM((s, LANES), _F32), pltpu.VMEM((s, LANES), _F32)],
        compiler_params=_params(("parallel", "parallel"), 56),
        name="dilated_attention",
    )(_attn_bias_tiles(rel_bias), qkv, qkv, qkv)


FFT_COLS = 512
FFT_K1 = 16


def _split(x):
    hi = x.astype(_BF16)
    return hi, (x - hi.astype(_F32)).astype(_BF16)


def _dot3(a, b):
    dot = functools.partial(jnp.dot, preferred_element_type=_F32)
    return dot(a[0], b[0]) + dot(a[1], b[0]) + dot(a[0], b[1])


def _fft_stage1_kernel(fh_ref, fl_ref, x_ref, ar_ref, ai_ref):
    n1 = x_ref.shape[0]
    f = (fh_ref[...], fl_ref[...])
    for j in range(x_ref.shape[1]):
        r = _dot3(f, _split(x_ref[:, j, :]))
        ar_ref[:, j, :] = r[:n1]
        ai_ref[:, j, :] = r[n1:]


def _fft_stage2_kernel(twc_ref, tws_ref, c2h, c2l, s2h, s2l, cch, ccl, sch, scl, ar_ref, ai_ref, o_ref, rows_f32):
    n2, gd = c2h.shape[0], cch.shape[0]
    c2, s2 = (c2h[...], c2l[...]), (s2h[...], s2l[...])
    cc, sc = (cch[...], ccl[...]), (sch[...], scl[...])
    for j in range(o_ref.shape[1]):
        rows = slice(j * n2, (j + 1) * n2)
        ar, ai = ar_ref[rows, :], ai_ref[rows, :]
        c, s = twc_ref[rows, :], tws_ref[rows, :]
        pr, pi = _split(ar * c + ai * s), _split(ai * c - ar * s)
        zr = _dot3(c2, pr) + _dot3(s2, pi)
        zi = _dot3(c2, pi) - _dot3(s2, pr)
        for g in range(ar.shape[1] // gd):
            cols = slice(g * gd, (g + 1) * gd)
            rows_f32[:, j, cols] = _dot3(_split(zr[:, cols]), cc) + _dot3(_split(zi[:, cols]), sc)
    o_ref[...] = rows_f32[...].astype(o_ref.dtype)


def _fourier_mix(u):
    b, s, width = u.shape
    gd = FNET_GROUP_DIM
    n2 = 128
    n1 = s // n2
    nk = min(FFT_K1, n1)
    cols = min(FFT_COLS, width)
    assert n1 * n2 == s and n1 % nk == 0 and nk % 16 == 0 and width % cols == 0 and cols % gd == 0

    def trig(n, rows, cols_):
        ang = 2.0 * np.pi * ((np.arange(rows)[:, None] * np.arange(cols_)[None, :]) % n) / n
        return np.cos(ang), np.sin(ang)

    def pair(x):
        return _split(jnp.asarray(x, _F32))

    c1, s1 = trig(n1, n1, n1)
    f1h, f1l = pair(np.concatenate([c1, -s1], axis=0))
    tc, tsn = trig(s, n1, n2)
    twc = jnp.asarray(tc.reshape(s, 1), _F32)
    tws = jnp.asarray(tsn.reshape(s, 1), _F32)
    c2, s2 = trig(n2, n2, n2)
    norm = 1.0 / math.sqrt(s * gd)
    cc, sc = trig(gd, gd, gd)
    consts = [*pair(c2), *pair(s2), *pair(cc * norm), *pair(sc * norm)]

    x_spec = pl.BlockSpec((None, n1, SUBLANES, width), lambda bi, j: (bi, 0, j, 0))
    ar, ai = pl.pallas_call(
        _fft_stage1_kernel,
        grid=(b, n2 // SUBLANES),
        in_specs=[pl.BlockSpec((2 * n1, n1), lambda bi, j: (0, 0))] * 2 + [x_spec],
        out_specs=[x_spec, x_spec],
        out_shape=[jax.ShapeDtypeStruct((b, n1, n2, width), _F32)] * 2,
        compiler_params=_params(("parallel", "parallel"), 40),
        name="fft_stage1",
    )(f1h, f1l, u.reshape(b, n1, n2, width))

    tw_spec = pl.BlockSpec((nk * n2, 1), lambda bi, i, c: (i, 0))
    m_spec = pl.BlockSpec((n2, n2), lambda bi, i, c: (0, 0))
    g_spec = pl.BlockSpec((gd, gd), lambda bi, i, c: (0, 0))
    z_spec = pl.BlockSpec((None, nk * n2, cols), lambda bi, i, c: (bi, i, c))
    out = pl.pallas_call(
        _fft_stage2_kernel,
        grid=(b, n1 // nk, width // cols),
        in_specs=[tw_spec, tw_spec] + [m_spec] * 4 + [g_spec] * 4 + [z_spec, z_spec],
        out_specs=pl.BlockSpec((None, n2, nk, cols), lambda bi, i, c: (bi, 0, i, c)),
        out_shape=jax.ShapeDtypeStruct((b, n2, n1, width), _BF16),
        scratch_shapes=[pltpu.VMEM((n2, nk, cols), _F32)],
        compiler_params=_params(("parallel", "parallel", "parallel"), 48),
        name="fft_stage2",
    )(twc, tws, *consts, ar.reshape(b, s, width), ai.reshape(b, s, width))
    return out.reshape(b, s, width)


CONV_HALO = 16
CONV_ROWS = 64


def _glu_conv_kernel(w_ref, b_ref, vc_ref, vp_ref, vn_ref, gc_ref, gp_ref, gn_ref, y_ref, ext, shifted, *, width):
    i, last = pl.program_id(1), pl.num_programs(1) - 1
    ts, tc = vc_ref.shape[1], vc_ref.shape[2]
    halo, pad = CONV_HALO, (width - 1) // 2

    def glu(v_ref, g_ref):
        return v_ref[0] * jax.nn.sigmoid(g_ref[0])

    ext[0:halo, :] = jnp.where(i > 0, glu(vp_ref, gp_ref), 0.0)
    ext[halo:halo + ts, :] = glu(vc_ref, gc_ref)
    ext[halo + ts:, :] = jnp.where(i < last, glu(vn_ref, gn_ref), 0.0)
    rows = min(CONV_ROWS, ts)
    span = shifted.shape[1]
    for cb in range(tc // LANES):
        cols = slice(cb * LANES, (cb + 1) * LANES)
        for k in range(1, SUBLANES):
            shifted[k - 1, :, cols] = ext[k:k + span, cols]
        for r0 in range(0, ts, rows):
            acc = jnp.broadcast_to(b_ref[:, cols], (rows, LANES))
            for t in range(width):
                start = r0 + halo - pad + t
                k, base = start % SUBLANES, start - start % SUBLANES
                tap = ext[base:base + rows, cols] if k == 0 else shifted[k - 1, base:base + rows, cols]
                acc = acc + w_ref[t:t + 1, cols] * tap
            y_ref[0, r0:r0 + rows, cols] = acc


def _glu_conv(proj, conv_w, conv_b):
    b, s, _ = proj.shape
    width, cdim = conv_w.shape
    assert (width - 1) // 2 <= CONV_HALO
    ts = _pick(s, (256, 128))
    tc = _pick(cdim, (512, 256, 128))
    ncb, hb = cdim // tc, ts // CONV_HALO
    nhalo = s // CONV_HALO
    w_pad = jnp.pad(conv_w, ((0, -width % 8), (0, 0)))

    def cur(off):
        return pl.BlockSpec((1, ts, tc), lambda bi, i, c: (bi, i, off * ncb + c))

    def prev(off):
        return pl.BlockSpec((1, CONV_HALO, tc), lambda bi, i, c: (bi, jnp.maximum(i * hb - 1, 0), off * ncb + c))

    def nxt(off):
        return pl.BlockSpec((1, CONV_HALO, tc),
                            lambda bi, i, c: (bi, jnp.minimum((i + 1) * hb, nhalo - 1), off * ncb + c))

    return pl.pallas_call(
        functools.partial(_glu_conv_kernel, width=width),
        grid=(b, s // ts, ncb),
        in_specs=[pl.BlockSpec((w_pad.shape[0], tc), lambda bi, i, c: (0, c)),
                  pl.BlockSpec((1, tc), lambda bi, i, c: (0, c)),
                  cur(0), prev(0), nxt(0), cur(1), prev(1), nxt(1)],
        out_specs=pl.BlockSpec((1, ts, tc), lambda bi, i, c: (bi, i, c)),
        out_shape=jax.ShapeDtypeStruct((b, s, cdim), _F32),
        scratch_shapes=[pltpu.VMEM((ts + 2 * CONV_HALO, tc), _F32),
                        pltpu.VMEM((SUBLANES - 1, ts + 2 * CONV_HALO - SUBLANES, tc), _F32)],
        compiler_params=_params(("parallel", "parallel", "parallel"), 32),
        name="glu_conv",
    )(w_pad, conv_b.reshape(1, cdim), proj, proj, proj, proj, proj, proj)


SHORT_HALO = 8


def _norm_gate_kernel(y_ref, lg_ref, lb_ref, w_ref, db_ref, cc_ref, cp_ref, cn_ref, hc_ref, hp_ref, hn_ref,
                      o_ref, ext, *, width):
    i, last = pl.program_id(1), pl.num_programs(1) - 1
    ts, cdim = y_ref.shape[1], y_ref.shape[2]
    ddim = db_ref.shape[2]
    halo, pad = SHORT_HALO, (width - 1) // 2
    rows = min(CONV_ROWS, ts)

    for r0 in range(0, ts, rows):
        y = y_ref[0, r0:r0 + rows, :]
        yc = y - jnp.mean(y, axis=-1, keepdims=True)
        inv = lax.rsqrt(jnp.mean(yc * yc, axis=-1, keepdims=True) + NORM_EPS)
        n = yc * inv * lg_ref[...] + lb_ref[...]
        o_ref[0, r0:r0 + rows, 0:cdim] = (n * jax.nn.sigmoid(n)).astype(o_ref.dtype)

    ext[0:halo, :] = jnp.where(i > 0, cp_ref[0] * hp_ref[0], 0.0)
    ext[halo:halo + ts, :] = cc_ref[0] * hc_ref[0]
    ext[halo + ts:, :] = jnp.where(i < last, cn_ref[0] * hn_ref[0], 0.0)
    for cb in range(ddim // LANES):
        cols = slice(cb * LANES, (cb + 1) * LANES)
        for r0 in range(0, ts, rows):
            acc = jnp.zeros((rows, LANES), _F32)
            for t in range(width):
                start = r0 + halo - pad + t
                acc = acc + w_ref[t:t + 1, cols] * ext[start:start + rows, cols]
            o_ref[0, r0:r0 + rows, cdim + cb * LANES:cdim + (cb + 1) * LANES] = (
                db_ref[0, r0:r0 + rows, cols] * acc).astype(o_ref.dtype)


def _norm_gate(y, proj, ln_g, ln_b, conv_d_w):
    b, s, cdim = y.shape
    width, ddim = conv_d_w.shape
    assert (width - 1) // 2 <= SHORT_HALO and cdim % ddim == 0
    ts = _pick(s, (256, 128))
    hb, nhalo = ts // SHORT_HALO, s // SHORT_HALO
    base = 2 * cdim // ddim
    w_pad = jnp.pad(conv_d_w, ((0, -width % 8), (0, 0)))

    def cur(off):
        return pl.BlockSpec((1, ts, ddim), lambda bi, i: (bi, i, base + off))

    def prev(off):
        return pl.BlockSpec((1, SHORT_HALO, ddim), lambda bi, i: (bi, jnp.maximum(i * hb - 1, 0), base + off))

    def nxt(off):
        return pl.BlockSpec((1, SHORT_HALO, ddim),
                            lambda bi, i: (bi, jnp.minimum((i + 1) * hb, nhalo - 1), base + off))

    vec = pl.BlockSpec((1, cdim), lambda bi, i: (0, 0))
    return pl.pallas_call(
        functools.partial(_norm_gate_kernel, width=width),
        grid=(b, s // ts),
        in_specs=[pl.BlockSpec((1, ts, cdim), lambda bi, i: (bi, i, 0)), vec, vec,
                  pl.BlockSpec((w_pad.shape[0], ddim), lambda bi, i: (0, 0)),
                  cur(0), cur(1), prev(1), nxt(1), cur(2), prev(2), nxt(2)],
        out_specs=pl.BlockSpec((1, ts, cdim + ddim), lambda bi, i: (bi, i, 0)),
        out_shape=jax.ShapeDtypeStruct((b, s, cdim + ddim), _BF16),
        scratch_shapes=[pltpu.VMEM((ts + 2 * SHORT_HALO, ddim), _F32)],
        compiler_params=_params(("parallel", "parallel"), 48),
        name="norm_gate",
    )(y, ln_g.reshape(1, cdim), ln_b.reshape(1, cdim), w_pad, proj, proj, proj, proj, proj, proj, proj)


def _mixer_ab(x, h, stats, w_in, w_out, index, rel_bias, b, s, next_gain):
    qkv_dim = HEADS_A * HEAD_DIM
    qkv = _scaled_matmul(h, stats, w_in, index, 0, 3 * qkv_dim, _BF16, "in_ab_qkv")
    u = _scaled_matmul(h, stats, w_in, index, 3 * qkv_dim, w_in.shape[2] - 3 * qkv_dim, _F32, "in_ab_u")
    a_out = _dilated_mixture(qkv.reshape(b, s, 3 * qkv_dim), rel_bias).reshape(b * s, qkv_dim)
    b_out = _fourier_mix(u.reshape(b, s, -1)).reshape(b * s, -1)
    return _matmul_resid([a_out, b_out], w_out[index].astype(_BF16), x, 1.0, "out_ab", gain=next_gain)


def _mixer_cd(x, h, stats, w_in, conv_c_w, conv_c_b, ln_c_g, ln_c_b, conv_d_w, w_out, index, b, s, next_gain):
    proj = _scaled_matmul(h, stats, w_in, index, 0, w_in.shape[2], _F32, "in_cd").reshape(b, s, -1)
    y = _glu_conv(proj, conv_c_w[index], conv_c_b[index])
    mixed = _norm_gate(y, proj, ln_c_g[index], ln_c_b[index], conv_d_w[index]).reshape(b * s, -1)
    return _matmul_resid([mixed], w_out[index].astype(_BF16), x, 1.0, "out_cd", gain=next_gain)


def kernel(x, ffn1_norm, ffn1_w_gate, ffn1_w_up, ffn1_w_down, mix_norm, ffn2_norm, ffn2_w_gate, ffn2_w_up,
           ffn2_w_down, rel_bias, w_in_ab, w_out_ab, w_in_cd, conv_c_w, conv_c_b, ln_c_g, ln_c_b, conv_d_w,
           w_out_cd, final_norm):
    b, s, d = x.shape
    depth = ffn1_norm.shape[0]
    x = x.reshape(b * s, d)
    h, stats = _rmsnorm(x, ffn1_norm[0], _BF16, "ffn1_l0_norm"), None
    for layer in range(depth):
        x, h, stats = _ffn(x, h, stats, ffn1_w_gate, ffn1_w_up, ffn1_w_down, layer, f"ffn1_l{layer}",
                           mix_norm[layer])
        i = layer // 2
        if layer % 2 == 0:
            x, h, stats = _mixer_ab(x, h, stats, w_in_ab, w_out_ab, i, rel_bias, b, s, ffn2_norm[layer])
        else:
            x, h, stats = _mixer_cd(x, h, stats, w_in_cd, conv_c_w, conv_c_b, ln_c_g, ln_c_b, conv_d_w,
                                    w_out_cd, i, b, s, ffn2_norm[layer])
        if layer + 1 < depth:
            x, h, stats = _ffn(x, h, stats, ffn2_w_gate, ffn2_w_up, ffn2_w_down, layer, f"ffn2_l{layer}",
                               ffn1_norm[layer + 1])
        else:
            x = _ffn(x, h, stats, ffn2_w_gate, ffn2_w_up, ffn2_w_down, layer, f"ffn2_l{layer}", None)
    return _rmsnorm(x, final_norm, _F32, "final_norm").reshape(b, s, d)
```

```python
import functools
import math

import jax
import jax.numpy as jnp
import numpy as np
from jax import lax
from jax.experimental import pallas as pl
from jax.experimental.pallas import tpu as pltpu

HEAD_DIM = 128
HEADS_A = 24
A_BRANCHES = ((128, 1), (512, 4), (2048, 16))
FNET_GROUPS = 8
FNET_GROUP_DIM = 128
REL_BUCKETS = 32
REL_MAX_DISTANCE = 1024
CONV_C_DIM = 2048
CONV_D_DIM = 2048
FFN_RESIDUAL = 0.5
NORM_EPS = 1e-6
NEG_INF = -1e30

LANES = 128
SUBLANES = 8
ATT_TQ = 128
ATT_HALO = 64
ATT_UNROLL = 16
ATT_OUT_ROWS = 256
ATT_CHUNK = 2048
NORM_EMIT_ROWS = 256
MIB = 1024 * 1024

_F32 = jnp.float32
_BF16 = jnp.bfloat16


def _params(sem, vmem_mib):
    return pltpu.CompilerParams(dimension_semantics=sem, vmem_limit_bytes=vmem_mib * MIB)


def _pick(n, prefs):
    for p in prefs:
        if n % p == 0:
            return p
    return n


def _rms(x, g):
    inv = lax.rsqrt(jnp.mean(x * x, axis=-1, keepdims=True) + NORM_EPS)
    return x * inv * g


def _col_to_row(col):
    parts = [jnp.transpose(jnp.broadcast_to(col[k:k + LANES], (LANES, LANES)))[0:1, :]
             for k in range(0, col.shape[0], LANES)]
    return parts[0] if len(parts) == 1 else jnp.concatenate(parts, axis=1)


def _inv_rms(ss_ref, d):
    tot = ss_ref[0]
    for k in range(1, ss_ref.shape[0]):
        tot = tot + ss_ref[k]
    return lax.rsqrt(tot / d + NORM_EPS)


def _row_scale(r, inv_row):
    reps = r.shape[1] // LANES
    blocks = []
    for k in range(0, r.shape[0], LANES):
        bc = jnp.transpose(jnp.broadcast_to(inv_row[:, k:k + LANES], (LANES, LANES)))
        blocks.append(r[k:k + LANES] * (bc if reps == 1 else jnp.concatenate([bc] * reps, axis=1)))
    return blocks[0] if len(blocks) == 1 else jnp.concatenate(blocks, axis=0)


def _rmsnorm_kernel(x_ref, g_ref, o_ref):
    o_ref[...] = _rms(x_ref[...], g_ref[...]).astype(o_ref.dtype)


def _rmsnorm(x, g, out_dtype, name):
    m, d = x.shape
    tm = _pick(m, (256, 128))
    return pl.pallas_call(
        _rmsnorm_kernel,
        grid=(m // tm,),
        in_specs=[pl.BlockSpec((tm, d), lambda i: (i, 0)), pl.BlockSpec((1, d), lambda i: (0, 0))],
        out_specs=pl.BlockSpec((tm, d), lambda i: (i, 0)),
        out_shape=jax.ShapeDtypeStruct((m, d), out_dtype),
        compiler_params=_params(("parallel",), 32),
        name=name,
    )(x, g.reshape(1, d))


def _scaled_matmul_kernel(h_ref, ss_ref, w_ref, o_ref, wbf):
    @pl.when(pl.program_id(1) == 0)
    def _():
        wbf[...] = w_ref[0].astype(wbf.dtype)

    r = jnp.dot(h_ref[...], wbf[...], preferred_element_type=_F32)
    o_ref[...] = _row_scale(r, _inv_rms(ss_ref, h_ref.shape[1])).astype(o_ref.dtype)


def _scaled_matmul(h, stats, w, index, col0, width, out_dtype, name):
    m, d = h.shape
    tm = _pick(m, (1024, 512, 256, 128))
    tn = _pick(math.gcd(col0, width), (512, 256, 128))
    return pl.pallas_call(
        _scaled_matmul_kernel,
        grid=(width // tn, m // tm),
        in_specs=[pl.BlockSpec((tm, d), lambda j, i: (i, 0)),
                  pl.BlockSpec((stats.shape[0], 1, tm), lambda j, i: (0, 0, i)),
                  pl.BlockSpec((1, d, tn), lambda j, i: (index, 0, col0 // tn + j))],
        out_specs=pl.BlockSpec((tm, tn), lambda j, i: (i, j)),
        out_shape=jax.ShapeDtypeStruct((m, width), out_dtype),
        scratch_shapes=[pltpu.VMEM((d, tn), _BF16)],
        compiler_params=_params(("arbitrary", "arbitrary"), 56),
        name=name,
    )(h, stats, w)


def _matmul_resid_kernel(*refs, n, scale, emit_norm):
    a_refs, w_refs, x_ref = refs[:n], refs[n:2 * n], refs[2 * n]

    def residual(rows):
        acc = jnp.dot(a_refs[0][rows, :], w_refs[0][...], preferred_element_type=_F32)
        for a_ref, w_ref in zip(a_refs[1:], w_refs[1:]):
            acc = acc + jnp.dot(a_ref[rows, :], w_ref[...], preferred_element_type=_F32)
        return x_ref[rows, :] + (acc if scale == 1.0 else scale * acc)

    tm = x_ref.shape[0]
    if not emit_norm:
        refs[2 * n + 1][...] = residual(slice(0, tm))
        return
    g_ref, o_ref, xg_ref, ss_ref = refs[2 * n + 1:]
    sub = NORM_EMIT_ROWS if tm % NORM_EMIT_ROWS == 0 else tm
    for r0 in range(0, tm, sub):
        rows = slice(r0, r0 + sub)
        y = residual(rows)
        o_ref[rows, :] = y
        xg_ref[rows, :] = (y * g_ref[...]).astype(xg_ref.dtype)
        ss_ref[0, :, rows] = _col_to_row(jnp.sum(y * y, axis=1, keepdims=True))


def _matmul_resid(pieces, w, x, scale, name, gain=None):
    m, n = x.shape
    ktot = w.shape[0]
    tm = _pick(m, (1024, 512, 256, 128)) if ktot <= 4096 else _pick(m, (512, 256, 128))
    tn = _pick(n, (512, 256, 128))
    a_specs, w_specs, row = [], [], 0
    for a in pieces:
        kp = a.shape[1]
        assert row % kp == 0
        a_specs.append(pl.BlockSpec((tm, kp), lambda i, j: (i, 0)))
        w_specs.append(pl.BlockSpec((kp, tn), lambda i, j, rb=row // kp: (rb, j)))
        row += kp
    assert row == ktot
    xo_spec = pl.BlockSpec((tm, tn), lambda i, j: (i, j))
    in_specs, out_specs, args = a_specs + w_specs + [xo_spec], xo_spec, [*pieces, *([w] * len(pieces)), x]
    out_shape = jax.ShapeDtypeStruct((m, n), _F32)
    if gain is not None:
        in_specs.append(pl.BlockSpec((1, tn), lambda i, j: (0, j)))
        args.append(gain.reshape(1, n))
        out_specs = [xo_spec, xo_spec, pl.BlockSpec((1, 1, tm), lambda i, j: (j, 0, i))]
        out_shape = [out_shape, jax.ShapeDtypeStruct((m, n), _BF16), jax.ShapeDtypeStruct((n // tn, 1, m), _F32)]
    return pl.pallas_call(
        functools.partial(_matmul_resid_kernel, n=len(pieces), scale=scale, emit_norm=gain is not None),
        grid=(m // tm, n // tn),
        in_specs=in_specs,
        out_specs=out_specs,
        out_shape=out_shape,
        compiler_params=_params(("parallel", "parallel"), 56),
        name=name,
    )(*args)


def _ffn_up_kernel(h_ref, *refs, scaled):
    ss_ref = refs[0] if scaled else None
    wg_ref, wu_ref, wd_ref, o_ref, wdo_ref, wcat = refs[1:] if scaled else refs
    tn = wg_ref.shape[2]

    @pl.when(pl.program_id(1) == 0)
    def _():
        wcat[:, :tn] = wg_ref[0].astype(wcat.dtype)
        wcat[:, tn:] = wu_ref[0].astype(wcat.dtype)
        wdo_ref[...] = wd_ref[0].astype(wdo_ref.dtype)

    r = jnp.dot(h_ref[...], wcat[...], preferred_element_type=_F32)
    if scaled:
        r = _row_scale(r, _inv_rms(ss_ref, h_ref.shape[1]))
    a, b = r[:, :tn], r[:, tn:]
    o_ref[...] = (a * jax.nn.sigmoid(a) * b).astype(o_ref.dtype)


def _ffn_up(h, stats, w_gate, w_up, w_down, layer, name):
    m, d = h.shape
    f = w_gate.shape[2]
    tm = _pick(m, (1024, 512, 256, 128))
    tn = _pick(f, (256, 128))
    w_spec = pl.BlockSpec((1, d, tn), lambda j, i: (layer, 0, j))
    stat_specs = [] if stats is None else [pl.BlockSpec((stats.shape[0], 1, tm), lambda j, i: (0, 0, i))]
    return pl.pallas_call(
        functools.partial(_ffn_up_kernel, scaled=stats is not None),
        grid=(f // tn, m // tm),
        in_specs=[pl.BlockSpec((tm, d), lambda j, i: (i, 0))] + stat_specs + [
            w_spec, w_spec, pl.BlockSpec((1, tn, d), lambda j, i: (layer, j, 0))],
        out_specs=[pl.BlockSpec((tm, tn), lambda j, i: (i, j)),
                   pl.BlockSpec((tn, d), lambda j, i: (j, 0))],
        out_shape=[jax.ShapeDtypeStruct((m, f), _BF16), jax.ShapeDtypeStruct((f, d), _BF16)],
        scratch_shapes=[pltpu.VMEM((d, 2 * tn), _BF16)],
        compiler_params=_params(("arbitrary", "arbitrary"), 56),
        name=name,
    )(h, *([] if stats is None else [stats]), w_gate, w_up, w_down)


def _ffn(x, h, stats, w_gate, w_up, w_down, layer, name, next_gain):
    hidden, wd = _ffn_up(h, stats, w_gate, w_up, w_down, layer, name + "_up")
    return _matmul_resid([hidden], wd, x, FFN_RESIDUAL, name + "_down", gain=next_gain)


def _t5_bucket_np(rel):
    half = REL_BUCKETS // 2
    max_exact = half // 2
    ret = np.where(rel > 0, half, 0)
    n = np.abs(rel)
    nf = np.maximum(n, 1).astype(np.float64)
    val = np.log(nf / max_exact) / math.log(REL_MAX_DISTANCE / max_exact) * (half - max_exact)
    large = np.minimum(max_exact + val.astype(np.int64), half - 1)
    return (ret + np.where(n < max_exact, n, large)).astype(np.int32)


def _branch_dilations():
    return tuple(sorted((d for _, d in A_BRANCHES), reverse=True))


def _attn_bias_tiles(rel_bias):
    tq, halo = ATT_TQ, ATT_HALO
    win = tq + 2 * halo
    period = 2 * win
    c = np.arange(win)[None, :]
    ok_first = np.broadcast_to(c >= halo, (tq, win))
    ok_last = np.broadcast_to(c < tq + halo, (tq, win))
    branches = []
    for dilation in _branch_dilations():
        rel = np.arange(-halo, halo + 1)
        band = rel_bias[_t5_bucket_np(rel * dilation)].astype(_F32).T
        heads = band.shape[0]
        row = jnp.full((heads, period), NEG_INF, _F32).at[:, :2 * halo + 1].set(band)
        toeplitz = jnp.tile(row, (1, tq))[:, :tq * (period - 1)].reshape(heads, tq, period - 1)[:, :, :win]
        branches.append(jnp.stack(
            [jnp.where(mk[None], toeplitz, NEG_INF)
             for mk in (np.ones((tq, win), bool), ok_first, ok_last, ok_first & ok_last)], axis=1))
    return jnp.stack(branches, axis=1)


def _attn_kernel(bm_ref, q_ref, k_ref, v_ref, o_ref, wide, qd, kd, vd, acc, mrun, lrun, *, scale, dilations):
    seq = q_ref.shape[1]
    tq, halo = ATT_TQ, ATT_HALO
    win = tq + 2 * halo
    ntiles = seq // tq
    zeros = jnp.zeros((halo, HEAD_DIM), kd.dtype)
    ones = jnp.ones((win, LANES), vd.dtype)

    for g, d in enumerate(dilations):
        sub = seq // d
        nt = sub // tq
        chunk = wide.shape[0]
        per = chunk // d
        for src, dst, padded in ((q_ref, qd, False), (k_ref, kd, True), (v_ref, vd, True)):
            if d == 1 and not padded:
                continue
            stride_r = sub + 2 * halo if padded else sub
            first = halo if padded else 0
            for c in range(seq // chunk):
                def move(i, carry, c=c, src=src, dst=dst, first=first, widen=d > 1):
                    off = pl.multiple_of(i * ATT_OUT_ROWS, ATT_OUT_ROWS)
                    rows = src[0, pl.ds(c * chunk + off, ATT_OUT_ROWS), :]
                    if widen:
                        wide[pl.ds(off, ATT_OUT_ROWS), :] = rows.astype(_F32)
                    else:
                        dst[pl.ds(pl.multiple_of(first + c * chunk + off, ATT_HALO), ATT_OUT_ROWS), :] = rows
                    return carry

                lax.fori_loop(0, chunk // ATT_OUT_ROWS, move, 0)
                if d == 1:
                    continue
                for r in range(d):
                    base = r * stride_r + first + c * per
                    dst[base:base + per, :] = wide[pl.ds(r, per, stride=d), :].astype(dst.dtype)
            if padded:
                for r in range(d):
                    dst[r * stride_r:r * stride_r + halo, :] = zeros
                    dst[r * stride_r + halo + sub:(r + 1) * stride_r, :] = zeros

        def tile(tile_idx, g=g, d=d, nt=nt):
            r = tile_idx // nt
            t = tile_idx - r * nt
            q0 = pl.multiple_of(tile_idx * tq, tq)
            k0 = pl.multiple_of(tile_idx * tq + r * (2 * halo), 2 * halo)
            q = q_ref[0, pl.ds(q0, tq), :] if d == 1 else qd[pl.ds(q0, tq), :]
            kw = kd[pl.ds(k0, win), :]
            vw = vd[pl.ds(k0, win), :]
            s = lax.dot_general(q, kw, (((1,), (1,)), ((), ())), preferred_element_type=_F32)
            variant = jnp.where(t == 0, 1, 0) + jnp.where(t == nt - 1, 2, 0)
            s = s * scale + bm_ref[0, g, variant]
            mx = jnp.max(s, axis=-1, keepdims=True)
            p = jnp.exp(s - mx)
            pv = jnp.dot(p.astype(vw.dtype), jnp.concatenate([vw, ones], axis=1), preferred_element_type=_F32)
            pv, den = pv[:, :HEAD_DIM], pv[:, HEAD_DIM:]
            idx = pl.ds(q0, tq) if d == 1 else pl.ds(t * (tq * d) + r, tq, stride=d)
            if g == 0:
                acc[idx, :] = pv
                mrun[idx, :] = jnp.broadcast_to(mx, (tq, LANES))
                lrun[idx, :] = den
            else:
                m_old = mrun[idx, :]
                m_new = jnp.maximum(m_old, mx)
                c_old = jnp.exp(m_old - m_new)
                c_new = jnp.exp(mx - m_new)
                acc[idx, :] = acc[idx, :] * c_old + pv * c_new
                lrun[idx, :] = lrun[idx, :] * c_old + den * c_new
                mrun[idx, :] = m_new

        def tiles(it, carry, tile=tile):
            for u in range(ATT_UNROLL):
                tile(it * ATT_UNROLL + u)
            return carry

        lax.fori_loop(0, ntiles // ATT_UNROLL, tiles, 0)

    def finish(it, carry):
        rows = pl.ds(pl.multiple_of(it * ATT_OUT_ROWS, ATT_OUT_ROWS), ATT_OUT_ROWS)
        o_ref[0, rows, :] = (acc[rows, :] / lrun[rows, :]).astype(o_ref.dtype)
        return carry

    lax.fori_loop(0, seq // ATT_OUT_ROWS, finish, 0)


def _dilated_mixture(qkv, rel_bias):
    b, s, _ = qkv.shape
    h, e = HEADS_A, HEAD_DIM
    dilations = _branch_dilations()
    for window, d in A_BRANCHES:
        assert window // (2 * d) == ATT_HALO and (s // d) % ATT_TQ == 0
    assert (s // ATT_TQ) % ATT_UNROLL == 0 and s % ATT_OUT_ROWS == 0
    chunk = min(ATT_CHUNK, s)
    assert s % chunk == 0 and chunk % ATT_OUT_ROWS == 0 and all((chunk // d) % 16 == 0 for d in dilations)
    win = ATT_TQ + 2 * ATT_HALO
    pad_rows = s + 2 * ATT_HALO * max(dilations)

    def in_spec(which):
        return pl.BlockSpec((1, s, e), lambda hi, bi: (bi, 0, which * h + hi))

    return pl.pallas_call(
        functools.partial(_attn_kernel, scale=1.0 / math.sqrt(e), dilations=dilations),
        grid=(h, b),
        in_specs=[pl.BlockSpec((1, len(dilations), 4, ATT_TQ, win), lambda hi, bi: (hi, 0, 0, 0, 0)),
                  in_spec(0), in_spec(1), in_spec(2)],
        out_specs=pl.BlockSpec((1, s, e), lambda hi, bi: (bi, 0, hi)),
        out_shape=jax.ShapeDtypeStruct((b, s, h * e), _BF16),
        scratch_shapes=[pltpu.VMEM((chunk, e), _F32), pltpu.VMEM((s, e), _BF16),
                        pltpu.VMEM((pad_rows, e), _BF16), pltpu.VMEM((pad_rows, e), _BF16),
                        pltpu.VMEM((s, e), _F32), pltpu.VMEM((s, LANES), _F32), pltpu.VMEM((s, LANES), _F32)],
        compiler_params=_params(("parallel", "parallel"), 56),
        name="dilated_attention",
    )(_attn_bias_tiles(rel_bias), qkv, qkv, qkv)


FFT_COLS = 512
FFT_K1 = 16


def _split(x):
    hi = x.astype(_BF16)
    return hi, (x - hi.astype(_F32)).astype(_BF16)


def _dot3(a, b):
    dot = functools.partial(jnp.dot, preferred_element_type=_F32)
    return dot(a[0], b[0]) + dot(a[1], b[0]) + dot(a[0], b[1])


def _fft_stage1_kernel(fh_ref, fl_ref, x_ref, ar_ref, ai_ref):
    n1, sub, width = x_ref.shape
    r = _dot3((fh_ref[...], fl_ref[...]), _split(x_ref[...].reshape(n1 * sub, width)))
    ar_ref[...] = r[:n1 * sub].reshape(n1, sub, width)
    ai_ref[...] = r[n1 * sub:].reshape(n1, sub, width)


def _fft_stage2_kernel(twc_ref, tws_ref, c2h, c2l, s2h, s2l, cch, ccl, sch, scl, ar_ref, ai_ref, o_ref, rows_f32):
    n2, gd = c2h.shape[0], cch.shape[0]
    c2, s2 = (c2h[...], c2l[...]), (s2h[...], s2l[...])
    cc, sc = (cch[...], ccl[...]), (sch[...], scl[...])
    for j in range(o_ref.shape[1]):
        rows = slice(j * n2, (j + 1) * n2)
        ar, ai = ar_ref[rows, :], ai_ref[rows, :]
        c, s = twc_ref[rows, :], tws_ref[rows, :]
        pr, pi = _split(ar * c + ai * s), _split(ai * c - ar * s)
        zr = _dot3(c2, pr) + _dot3(s2, pi)
        zi = _dot3(c2, pi) - _dot3(s2, pr)
        for g in range(ar.shape[1] // gd):
            cols = slice(g * gd, (g + 1) * gd)
            rows_f32[:, j, cols] = _dot3(_split(zr[:, cols]), cc) + _dot3(_split(zi[:, cols]), sc)
    o_ref[...] = rows_f32[...].astype(o_ref.dtype)


def _fourier_mix(u):
    b, s, width = u.shape
    gd = FNET_GROUP_DIM
    n2 = 128
    n1 = s // n2
    nk = min(FFT_K1, n1)
    cols = min(FFT_COLS, width)
    assert n1 * n2 == s and n1 % nk == 0 and nk % 16 == 0 and width % cols == 0 and cols % gd == 0

    def trig(n, rows, cols_):
        ang = 2.0 * np.pi * ((np.arange(rows)[:, None] * np.arange(cols_)[None, :]) % n) / n
        return np.cos(ang), np.sin(ang)

    def pair(x):
        return _split(jnp.asarray(x, _F32))

    c1, s1 = trig(n1, n1, n1)
    f1h, f1l = pair(np.kron(np.concatenate([c1, -s1], axis=0), np.eye(SUBLANES)))
    tc, tsn = trig(s, n1, n2)
    twc = jnp.asarray(tc.reshape(s, 1), _F32)
    tws = jnp.asarray(tsn.reshape(s, 1), _F32)
    c2, s2 = trig(n2, n2, n2)
    norm = 1.0 / math.sqrt(s * gd)
    cc, sc = trig(gd, gd, gd)
    consts = [*pair(c2), *pair(s2), *pair(cc * norm), *pair(sc * norm)]

    x_spec = pl.BlockSpec((None, n1, SUBLANES, width), lambda bi, j: (bi, 0, j, 0))
    ar, ai = pl.pallas_call(
        _fft_stage1_kernel,
        grid=(b, n2 // SUBLANES),
        in_specs=[pl.BlockSpec((2 * n1 * SUBLANES, n1 * SUBLANES), lambda bi, j: (0, 0))] * 2 + [x_spec],
        out_specs=[x_spec, x_spec],
        out_shape=[jax.ShapeDtypeStruct((b, n1, n2, width), _F32)] * 2,
        compiler_params=_params(("parallel", "parallel"), 40),
        name="fft_stage1",
    )(f1h, f1l, u.reshape(b, n1, n2, width))

    tw_spec = pl.BlockSpec((nk * n2, 1), lambda bi, i, c: (i, 0))
    m_spec = pl.BlockSpec((n2, n2), lambda bi, i, c: (0, 0))
    g_spec = pl.BlockSpec((gd, gd), lambda bi, i, c: (0, 0))
    z_spec = pl.BlockSpec((None, nk * n2, cols), lambda bi, i, c: (bi, i, c))
    out = pl.pallas_call(
        _fft_stage2_kernel,
        grid=(b, n1 // nk, width // cols),
        in_specs=[tw_spec, tw_spec] + [m_spec] * 4 + [g_spec] * 4 + [z_spec, z_spec],
        out_specs=pl.BlockSpec((None, n2, nk, cols), lambda bi, i, c: (bi, 0, i, c)),
        out_shape=jax.ShapeDtypeStruct((b, n2, n1, width), _BF16),
        scratch_shapes=[pltpu.VMEM((n2, nk, cols), _F32)],
        compiler_params=_params(("parallel", "parallel", "parallel"), 48),
        name="fft_stage2",
    )(twc, tws, *consts, ar.reshape(b, s, width), ai.reshape(b, s, width))
    return out.reshape(b, s, width)


CONV_HALO = 16
CONV_ROWS = 64


def _glu_conv_kernel(w_ref, b_ref, vc_ref, vp_ref, vn_ref, gc_ref, gp_ref, gn_ref, y_ref, ext, shifted, *, width):
    i, last = pl.program_id(1), pl.num_programs(1) - 1
    ts, tc = vc_ref.shape[1], vc_ref.shape[2]
    halo, pad = CONV_HALO, (width - 1) // 2

    def glu(v_ref, g_ref):
        return v_ref[0] * jax.nn.sigmoid(g_ref[0])

    ext[0:halo, :] = jnp.where(i > 0, glu(vp_ref, gp_ref), 0.0)
    ext[halo:halo + ts, :] = glu(vc_ref, gc_ref)
    ext[halo + ts:, :] = jnp.where(i < last, glu(vn_ref, gn_ref), 0.0)
    rows = min(CONV_ROWS, ts)
    span = shifted.shape[1]
    for cb in range(tc // LANES):
        cols = slice(cb * LANES, (cb + 1) * LANES)
        for k in range(1, SUBLANES):
            shifted[k - 1, :, cols] = ext[k:k + span, cols]
        for r0 in range(0, ts, rows):
            acc = jnp.broadcast_to(b_ref[:, cols], (rows, LANES))
            for t in range(width):
                start = r0 + halo - pad + t
                k, base = start % SUBLANES, start - start % SUBLANES
                tap = ext[base:base + rows, cols] if k == 0 else shifted[k - 1, base:base + rows, cols]
                acc = acc + w_ref[t:t + 1, cols] * tap
            y_ref[0, r0:r0 + rows, cols] = acc


def _glu_conv(proj, conv_w, conv_b):
    b, s, _ = proj.shape
    width, cdim = conv_w.shape
    assert (width - 1) // 2 <= CONV_HALO
    ts = _pick(s, (256, 128))
    tc = _pick(cdim, (512, 256, 128))
    ncb, hb = cdim // tc, ts // CONV_HALO
    nhalo = s // CONV_HALO
    w_pad = jnp.pad(conv_w, ((0, -width % 8), (0, 0)))

    def cur(off):
        return pl.BlockSpec((1, ts, tc), lambda bi, i, c: (bi, i, off * ncb + c))

    def prev(off):
        return pl.BlockSpec((1, CONV_HALO, tc), lambda bi, i, c: (bi, jnp.maximum(i * hb - 1, 0), off * ncb + c))

    def nxt(off):
        return pl.BlockSpec((1, CONV_HALO, tc),
                            lambda bi, i, c: (bi, jnp.minimum((i + 1) * hb, nhalo - 1), off * ncb + c))

    return pl.pallas_call(
        functools.partial(_glu_conv_kernel, width=width),
        grid=(b, s // ts, ncb),
        in_specs=[pl.BlockSpec((w_pad.shape[0], tc), lambda bi, i, c: (0, c)),
                  pl.BlockSpec((1, tc), lambda bi, i, c: (0, c)),
                  cur(0), prev(0), nxt(0), cur(1), prev(1), nxt(1)],
        out_specs=pl.BlockSpec((1, ts, tc), lambda bi, i, c: (bi, i, c)),
        out_shape=jax.ShapeDtypeStruct((b, s, cdim), _F32),
        scratch_shapes=[pltpu.VMEM((ts + 2 * CONV_HALO, tc), _F32),
                        pltpu.VMEM((SUBLANES - 1, ts + 2 * CONV_HALO - SUBLANES, tc), _F32)],
        compiler_params=_params(("parallel", "parallel", "parallel"), 32),
        name="glu_conv",
    )(w_pad, conv_b.reshape(1, cdim), proj, proj, proj, proj, proj, proj)


SHORT_HALO = 8


def _norm_gate_kernel(y_ref, lg_ref, lb_ref, w_ref, db_ref, cc_ref, cp_ref, cn_ref, hc_ref, hp_ref, hn_ref,
                      o_ref, ext, *, width):
    i, last = pl.program_id(1), pl.num_programs(1) - 1
    ts, cdim = y_ref.shape[1], y_ref.shape[2]
    ddim = db_ref.shape[2]
    halo, pad = SHORT_HALO, (width - 1) // 2
    rows = min(CONV_ROWS, ts)

    for r0 in range(0, ts, rows):
        y = y_ref[0, r0:r0 + rows, :]
        yc = y - jnp.mean(y, axis=-1, keepdims=True)
        inv = lax.rsqrt(jnp.mean(yc * yc, axis=-1, keepdims=True) + NORM_EPS)
        n = yc * inv * lg_ref[...] + lb_ref[...]
        o_ref[0, r0:r0 + rows, 0:cdim] = (n * jax.nn.sigmoid(n)).astype(o_ref.dtype)

    ext[0:halo, :] = jnp.where(i > 0, cp_ref[0] * hp_ref[0], 0.0)
    ext[halo:halo + ts, :] = cc_ref[0] * hc_ref[0]
    ext[halo + ts:, :] = jnp.where(i < last, cn_ref[0] * hn_ref[0], 0.0)
    for cb in range(ddim // LANES):
        cols = slice(cb * LANES, (cb + 1) * LANES)
        for r0 in range(0, ts, rows):
            acc = jnp.zeros((rows, LANES), _F32)
            for t in range(width):
                start = r0 + halo - pad + t
                acc = acc + w_ref[t:t + 1, cols] * ext[start:start + rows, cols]
            o_ref[0, r0:r0 + rows, cdim + cb * LANES:cdim + (cb + 1) * LANES] = (
                db_ref[0, r0:r0 + rows, cols] * acc).astype(o_ref.dtype)


def _norm_gate(y, proj, ln_g, ln_b, conv_d_w):
    b, s, cdim = y.shape
    width, ddim = conv_d_w.shape
    assert (width - 1) // 2 <= SHORT_HALO and cdim % ddim == 0
    ts = _pick(s, (256, 128))
    hb, nhalo = ts // SHORT_HALO, s // SHORT_HALO
    base = 2 * cdim // ddim
    w_pad = jnp.pad(conv_d_w, ((0, -width % 8), (0, 0)))

    def cur(off):
        return pl.BlockSpec((1, ts, ddim), lambda bi, i: (bi, i, base + off))

    def prev(off):
        return pl.BlockSpec((1, SHORT_HALO, ddim), lambda bi, i: (bi, jnp.maximum(i * hb - 1, 0), base + off))

    def nxt(off):
        return pl.BlockSpec((1, SHORT_HALO, ddim),
                            lambda bi, i: (bi, jnp.minimum((i + 1) * hb, nhalo - 1), base + off))

    vec = pl.BlockSpec((1, cdim), lambda bi, i: (0, 0))
    return pl.pallas_call(
        functools.partial(_norm_gate_kernel, width=width),
        grid=(b, s // ts),
        in_specs=[pl.BlockSpec((1, ts, cdim), lambda bi, i: (bi, i, 0)), vec, vec,
                  pl.BlockSpec((w_pad.shape[0], ddim), lambda bi, i: (0, 0)),
                  cur(0), cur(1), prev(1), nxt(1), cur(2), prev(2), nxt(2)],
        out_specs=pl.BlockSpec((1, ts, cdim + ddim), lambda bi, i: (bi, i, 0)),
        out_shape=jax.ShapeDtypeStruct((b, s, cdim + ddim), _BF16),
        scratch_shapes=[pltpu.VMEM((ts + 2 * SHORT_HALO, ddim), _F32)],
        compiler_params=_params(("parallel", "parallel"), 48),
        name="norm_gate",
    )(y, ln_g.reshape(1, cdim), ln_b.reshape(1, cdim), w_pad, proj, proj, proj, proj, proj, proj, proj)


def _mixer_ab(x, h, stats, w_in, w_out, index, rel_bias, b, s, next_gain):
    qkv_dim = HEADS_A * HEAD_DIM
    qkv = _scaled_matmul(h, stats, w_in, index, 0, 3 * qkv_dim, _BF16, "in_ab_qkv")
    u = _scaled_matmul(h, stats, w_in, index, 3 * qkv_dim, w_in.shape[2] - 3 * qkv_dim, _F32, "in_ab_u")
    a_out = _dilated_mixture(qkv.reshape(b, s, 3 * qkv_dim), rel_bias).reshape(b * s, qkv_dim)
    b_out = _fourier_mix(u.reshape(b, s, -1)).reshape(b * s, -1)
    return _matmul_resid([a_out, b_out], w_out[index].astype(_BF16), x, 1.0, "out_ab", gain=next_gain)


def _mixer_cd(x, h, stats, w_in, conv_c_w, conv_c_b, ln_c_g, ln_c_b, conv_d_w, w_out, index, b, s, next_gain):
    proj = _scaled_matmul(h, stats, w_in, index, 0, w_in.shape[2], _F32, "in_cd").reshape(b, s, -1)
    y = _glu_conv(proj, conv_c_w[index], conv_c_b[index])
    mixed = _norm_gate(y, proj, ln_c_g[index], ln_c_b[index], conv_d_w[index]).reshape(b * s, -1)
    return _matmul_resid([mixed], w_out[index].astype(_BF16), x, 1.0, "out_cd", gain=next_gain)


def kernel(x, ffn1_norm, ffn1_w_gate, ffn1_w_up, ffn1_w_down, mix_norm, ffn2_norm, ffn2_w_gate, ffn2_w_up,
           ffn2_w_down, rel_bias, w_in_ab, w_out_ab, w_in_cd, conv_c_w, conv_c_b, ln_c_g, ln_c_b, conv_d_w,
           w_out_cd, final_norm):
    b, s, d = x.shape
    depth = ffn1_norm.shape[0]
    x = x.reshape(b * s, d)
    h, stats = _rmsnorm(x, ffn1_norm[0], _BF16, "ffn1_l0_norm"), None
    for layer in range(depth):
        x, h, stats = _ffn(x, h, stats, ffn1_w_gate, ffn1_w_up, ffn1_w_down, layer, f"ffn1_l{layer}",
                           mix_norm[layer])
        i = layer // 2
        if layer % 2 == 0:
            x, h, stats = _mixer_ab(x, h, stats, w_in_ab, w_out_ab, i, rel_bias, b, s, ffn2_norm[layer])
        else:
            x, h, stats = _mixer_cd(x, h, stats, w_in_cd, conv_c_w, conv_c_b, ln_c_g, ln_c_b, conv_d_w,
                                    w_out_cd, i, b, s, ffn2_norm[layer])
        if layer + 1 < depth:
            x, h, stats = _ffn(x, h, stats, ffn2_w_gate, ffn2_w_up, ffn2_w_down, layer, f"ffn2_l{layer}",
                               ffn1_norm[layer + 1])
        else:
            x = _ffn(x, h, stats, ffn2_w_gate, ffn2_w_up, ffn2_w_down, layer, f"ffn2_l{layer}", None)
    return _rmsnorm(x, final_norm, _F32, "final_norm").reshape(b, s, d)
```

```python
import functools
import math

import jax
import jax.numpy as jnp
import numpy as np
from jax import lax
from jax.experimental import pallas as pl
from jax.experimental.pallas import tpu as pltpu

HEAD_DIM = 128
HEADS_A = 24
A_BRANCHES = ((128, 1), (512, 4), (2048, 16))
FNET_GROUPS = 8
FNET_GROUP_DIM = 128
REL_BUCKETS = 32
REL_MAX_DISTANCE = 1024
CONV_C_DIM = 2048
CONV_D_DIM = 2048
FFN_RESIDUAL = 0.5
NORM_EPS = 1e-6
NEG_INF = -1e30

LANES = 128
SUBLANES = 8
ATT_TQ = 128
ATT_HALO = 64
ATT_UNROLL = 16
ATT_OUT_ROWS = 256
ATT_CHUNK = 2048
NORM_EMIT_ROWS = 256
MIB = 1024 * 1024

_F32 = jnp.float32
_BF16 = jnp.bfloat16


def _params(sem, vmem_mib):
    return pltpu.CompilerParams(dimension_semantics=sem, vmem_limit_bytes=vmem_mib * MIB)


def _pick(n, prefs):
    for p in prefs:
        if n % p == 0:
            return p
    return n


def _rms(x, g):
    inv = lax.rsqrt(jnp.mean(x * x, axis=-1, keepdims=True) + NORM_EPS)
    return x * inv * g


def _col_to_row(col):
    parts = [jnp.transpose(jnp.broadcast_to(col[k:k + LANES], (LANES, LANES)))[0:1, :]
             for k in range(0, col.shape[0], LANES)]
    return parts[0] if len(parts) == 1 else jnp.concatenate(parts, axis=1)


def _inv_rms(ss_ref, d):
    tot = ss_ref[0]
    for k in range(1, ss_ref.shape[0]):
        tot = tot + ss_ref[k]
    return lax.rsqrt(tot / d + NORM_EPS)


def _row_scale(r, inv_row):
    reps = r.shape[1] // LANES
    blocks = []
    for k in range(0, r.shape[0], LANES):
        bc = jnp.transpose(jnp.broadcast_to(inv_row[:, k:k + LANES], (LANES, LANES)))
        blocks.append(r[k:k + LANES] * (bc if reps == 1 else jnp.concatenate([bc] * reps, axis=1)))
    return blocks[0] if len(blocks) == 1 else jnp.concatenate(blocks, axis=0)


def _rmsnorm_kernel(x_ref, g_ref, o_ref):
    o_ref[...] = _rms(x_ref[...], g_ref[...]).astype(o_ref.dtype)


def _rmsnorm(x, g, out_dtype, name):
    m, d = x.shape
    tm = _pick(m, (256, 128))
    return pl.pallas_call(
        _rmsnorm_kernel,
        grid=(m // tm,),
        in_specs=[pl.BlockSpec((tm, d), lambda i: (i, 0)), pl.BlockSpec((1, d), lambda i: (0, 0))],
        out_specs=pl.BlockSpec((tm, d), lambda i: (i, 0)),
        out_shape=jax.ShapeDtypeStruct((m, d), out_dtype),
        compiler_params=_params(("parallel",), 32),
        name=name,
    )(x, g.reshape(1, d))


def _scaled_matmul_kernel(h_ref, ss_ref, w_ref, o_ref, wbf):
    @pl.when(pl.program_id(1) == 0)
    def _():
        wbf[...] = w_ref[0].astype(wbf.dtype)

    r = jnp.dot(h_ref[...], wbf[...], preferred_element_type=_F32)
    o_ref[...] = _row_scale(r, _inv_rms(ss_ref, h_ref.shape[1])).astype(o_ref.dtype)


def _scaled_matmul(h, stats, w, index, col0, width, out_dtype, name):
    m, d = h.shape
    tm = _pick(m, (1024, 512, 256, 128))
    tn = _pick(math.gcd(col0, width), (512, 256, 128))
    return pl.pallas_call(
        _scaled_matmul_kernel,
        grid=(width // tn, m // tm),
        in_specs=[pl.BlockSpec((tm, d), lambda j, i: (i, 0)),
                  pl.BlockSpec((stats.shape[0], 1, tm), lambda j, i: (0, 0, i)),
                  pl.BlockSpec((1, d, tn), lambda j, i: (index, 0, col0 // tn + j))],
        out_specs=pl.BlockSpec((tm, tn), lambda j, i: (i, j)),
        out_shape=jax.ShapeDtypeStruct((m, width), out_dtype),
        scratch_shapes=[pltpu.VMEM((d, tn), _BF16)],
        compiler_params=_params(("arbitrary", "arbitrary"), 56),
        name=name,
    )(h, stats, w)


def _matmul_resid_kernel(*refs, n, scale, emit_norm):
    a_refs, w_refs, x_ref = refs[:n], refs[n:2 * n], refs[2 * n]

    def residual(rows):
        acc = jnp.dot(a_refs[0][rows, :], w_refs[0][...], preferred_element_type=_F32)
        for a_ref, w_ref in zip(a_refs[1:], w_refs[1:]):
            acc = acc + jnp.dot(a_ref[rows, :], w_ref[...], preferred_element_type=_F32)
        return x_ref[rows, :] + (acc if scale == 1.0 else scale * acc)

    tm = x_ref.shape[0]
    if not emit_norm:
        refs[2 * n + 1][...] = residual(slice(0, tm))
        return
    g_ref, o_ref, xg_ref, ss_ref = refs[2 * n + 1:]
    sub = NORM_EMIT_ROWS if tm % NORM_EMIT_ROWS == 0 else tm
    for r0 in range(0, tm, sub):
        rows = slice(r0, r0 + sub)
        y = residual(rows)
        o_ref[rows, :] = y
        xg_ref[rows, :] = (y * g_ref[...]).astype(xg_ref.dtype)
        ss_ref[0, :, rows] = _col_to_row(jnp.sum(y * y, axis=1, keepdims=True))


def _matmul_resid(pieces, w, x, scale, name, gain=None):
    m, n = x.shape
    ktot = w.shape[0]
    tm = _pick(m, (1024, 512, 256, 128)) if ktot <= 4096 else _pick(m, (512, 256, 128))
    tn = _pick(n, (512, 256, 128))
    a_specs, w_specs, row = [], [], 0
    for a in pieces:
        kp = a.shape[1]
        assert row % kp == 0
        a_specs.append(pl.BlockSpec((tm, kp), lambda i, j: (i, 0)))
        w_specs.append(pl.BlockSpec((kp, tn), lambda i, j, rb=row // kp: (rb, j)))
        row += kp
    assert row == ktot
    xo_spec = pl.BlockSpec((tm, tn), lambda i, j: (i, j))
    in_specs, out_specs, args = a_specs + w_specs + [xo_spec], xo_spec, [*pieces, *([w] * len(pieces)), x]
    out_shape = jax.ShapeDtypeStruct((m, n), _F32)
    if gain is not None:
        in_specs.append(pl.BlockSpec((1, tn), lambda i, j: (0, j)))
        args.append(gain.reshape(1, n))
        out_specs = [xo_spec, xo_spec, pl.BlockSpec((1, 1, tm), lambda i, j: (j, 0, i))]
        out_shape = [out_shape, jax.ShapeDtypeStruct((m, n), _BF16), jax.ShapeDtypeStruct((n // tn, 1, m), _F32)]
    return pl.pallas_call(
        functools.partial(_matmul_resid_kernel, n=len(pieces), scale=scale, emit_norm=gain is not None),
        grid=(m // tm, n // tn),
        in_specs=in_specs,
        out_specs=out_specs,
        out_shape=out_shape,
        compiler_params=_params(("parallel", "parallel"), 56),
        name=name,
    )(*args)


def _ffn_up_kernel(h_ref, *refs, scaled):
    ss_ref = refs[0] if scaled else None
    wg_ref, wu_ref, wd_ref, o_ref, wdo_ref, wcat = refs[1:] if scaled else refs
    j, i = pl.program_id(0), pl.program_id(1)
    rows, tn = wg_ref.shape[1], wg_ref.shape[2]
    slot = j % 2
    r0 = pl.multiple_of(i * rows, rows)
    wcat[slot, pl.ds(r0, rows), :tn] = wg_ref[0].astype(wcat.dtype)
    wcat[slot, pl.ds(r0, rows), tn:] = wu_ref[0].astype(wcat.dtype)
    wdo_ref[...] = wd_ref[0].astype(wdo_ref.dtype)

    @pl.when(j > 0)
    def _():
        r = jnp.dot(h_ref[...], wcat[1 - slot], preferred_element_type=_F32)
        if scaled:
            r = _row_scale(r, _inv_rms(ss_ref, h_ref.shape[1]))
        a, b = r[:, :tn], r[:, tn:]
        o_ref[...] = (a * jax.nn.sigmoid(a) * b).astype(o_ref.dtype)


def _ffn_up(h, stats, w_gate, w_up, w_down, layer, name):
    m, d = h.shape
    f = w_gate.shape[2]
    tm = _pick(m, (2048, 1024, 512, 256, 128))
    tn = _pick(f, (256, 128))
    ni, nblk = m // tm, f // tn
    assert d % ni == 0 and (tn // ni) % 16 == 0
    row_blk = lambda j, i: jnp.where(j == 0, 0, i)
    col_blk = lambda j: jnp.minimum(j, nblk - 1)
    wd_blk = lambda j, i: jnp.minimum(j * ni + i, nblk * ni - 1)
    w_spec = pl.BlockSpec((1, d // ni, tn), lambda j, i: (layer, i, col_blk(j)))
    stat_specs = [] if stats is None else [
        pl.BlockSpec((stats.shape[0], 1, tm), lambda j, i: (0, 0, row_blk(j, i)))]
    return pl.pallas_call(
        functools.partial(_ffn_up_kernel, scaled=stats is not None),
        grid=(nblk + 1, ni),
        in_specs=[pl.BlockSpec((tm, d), lambda j, i: (row_blk(j, i), 0))] + stat_specs + [
            w_spec, w_spec, pl.BlockSpec((1, tn // ni, d), lambda j, i: (layer, wd_blk(j, i), 0))],
        out_specs=[pl.BlockSpec((tm, tn), lambda j, i: (row_blk(j, i), jnp.maximum(j - 1, 0))),
                   pl.BlockSpec((tn // ni, d), lambda j, i: (wd_blk(j, i), 0))],
        out_shape=[jax.ShapeDtypeStruct((m, f), _BF16), jax.ShapeDtypeStruct((f, d), _BF16)],
        scratch_shapes=[pltpu.VMEM((2, d, 2 * tn), _BF16)],
        compiler_params=_params(("arbitrary", "arbitrary"), 58),
        name=name,
    )(h, *([] if stats is None else [stats]), w_gate, w_up, w_down)


def _ffn(x, h, stats, w_gate, w_up, w_down, layer, name, next_gain):
    hidden, wd = _ffn_up(h, stats, w_gate, w_up, w_down, layer, name + "_up")
    return _matmul_resid([hidden], wd, x, FFN_RESIDUAL, name + "_down", gain=next_gain)


def _t5_bucket_np(rel):
    half = REL_BUCKETS // 2
    max_exact = half // 2
    ret = np.where(rel > 0, half, 0)
    n = np.abs(rel)
    nf = np.maximum(n, 1).astype(np.float64)
    val = np.log(nf / max_exact) / math.log(REL_MAX_DISTANCE / max_exact) * (half - max_exact)
    large = np.minimum(max_exact + val.astype(np.int64), half - 1)
    return (ret + np.where(n < max_exact, n, large)).astype(np.int32)


def _branch_dilations():
    return tuple(sorted((d for _, d in A_BRANCHES), reverse=True))


def _attn_bias_tiles(rel_bias):
    tq, halo = ATT_TQ, ATT_HALO
    win = tq + 2 * halo
    period = 2 * win
    c = np.arange(win)[None, :]
    ok_first = np.broadcast_to(c >= halo, (tq, win))
    ok_last = np.broadcast_to(c < tq + halo, (tq, win))
    branches = []
    for dilation in _branch_dilations():
        rel = np.arange(-halo, halo + 1)
        band = rel_bias[_t5_bucket_np(rel * dilation)].astype(_F32).T
        heads = band.shape[0]
        row = jnp.full((heads, period), NEG_INF, _F32).at[:, :2 * halo + 1].set(band)
        toeplitz = jnp.tile(row, (1, tq))[:, :tq * (period - 1)].reshape(heads, tq, period - 1)[:, :, :win]
        branches.append(jnp.stack(
            [jnp.where(mk[None], toeplitz, NEG_INF)
             for mk in (np.ones((tq, win), bool), ok_first, ok_last, ok_first & ok_last)], axis=1))
    return jnp.stack(branches, axis=1)


def _attn_kernel(bm_ref, q_ref, k_ref, v_ref, o_ref, wide, qd, kd, vd, acc, mrun, lrun, *, scale, dilations):
    seq = q_ref.shape[1]
    tq, halo = ATT_TQ, ATT_HALO
    win = tq + 2 * halo
    ntiles = seq // tq
    zeros = jnp.zeros((halo, HEAD_DIM), kd.dtype)
    ones = jnp.ones((win, LANES), vd.dtype)

    for g, d in enumerate(dilations):
        sub = seq // d
        nt = sub // tq
        chunk = wide.shape[0]
        per = chunk // d
        for src, dst, padded in ((q_ref, qd, False), (k_ref, kd, True), (v_ref, vd, True)):
            if d == 1 and not padded:
                continue
            stride_r = sub + 2 * halo if padded else sub
            first = halo if padded else 0
            for c in range(seq // chunk):
                def move(i, carry, c=c, src=src, dst=dst, first=first, widen=d > 1):
                    off = pl.multiple_of(i * ATT_OUT_ROWS, ATT_OUT_ROWS)
                    rows = src[0, pl.ds(c * chunk + off, ATT_OUT_ROWS), :]
                    if widen:
                        wide[pl.ds(off, ATT_OUT_ROWS), :] = rows.astype(_F32)
                    else:
                        dst[pl.ds(pl.multiple_of(first + c * chunk + off, ATT_HALO), ATT_OUT_ROWS), :] = rows
                    return carry

                lax.fori_loop(0, chunk // ATT_OUT_ROWS, move, 0)
                if d == 1:
                    continue
                for r in range(d):
                    base = r * stride_r + first + c * per
                    dst[base:base + per, :] = wide[pl.ds(r, per, stride=d), :].astype(dst.dtype)
            if padded:
                for r in range(d):
                    dst[r * stride_r:r * stride_r + halo, :] = zeros
                    dst[r * stride_r + halo + sub:(r + 1) * stride_r, :] = zeros

        def tile(tile_idx, g=g, d=d, nt=nt):
            r = tile_idx // nt
            t = tile_idx - r * nt
            q0 = pl.multiple_of(tile_idx * tq, tq)
            k0 = pl.multiple_of(tile_idx * tq + r * (2 * halo), 2 * halo)
            q = q_ref[0, pl.ds(q0, tq), :] if d == 1 else qd[pl.ds(q0, tq), :]
            kw = kd[pl.ds(k0, win), :]
            vw = vd[pl.ds(k0, win), :]
            s = lax.dot_general(q, kw, (((1,), (1,)), ((), ())), preferred_element_type=_F32)
            variant = jnp.where(t == 0, 1, 0) + jnp.where(t == nt - 1, 2, 0)
            s = s * scale + bm_ref[0, g, variant]
            mx = jnp.max(s, axis=-1, keepdims=True)
            p = jnp.exp(s - mx)
            pv = jnp.dot(p.astype(vw.dtype), jnp.concatenate([vw, ones], axis=1), preferred_element_type=_F32)
            pv, den = pv[:, :HEAD_DIM], pv[:, HEAD_DIM:]
            idx = pl.ds(q0, tq) if d == 1 else pl.ds(t * (tq * d) + r, tq, stride=d)
            if g == 0:
                acc[idx, :] = pv
                mrun[idx, :] = jnp.broadcast_to(mx, (tq, LANES))
                lrun[idx, :] = den
            else:
                m_old = mrun[idx, :]
                m_new = jnp.maximum(m_old, mx)
                c_old = jnp.exp(m_old - m_new)
                c_new = jnp.exp(mx - m_new)
                acc[idx, :] = acc[idx, :] * c_old + pv * c_new
                lrun[idx, :] = lrun[idx, :] * c_old + den * c_new
                mrun[idx, :] = m_new

        def tiles(it, carry, tile=tile):
            for u in range(ATT_UNROLL):
                tile(it * ATT_UNROLL + u)
            return carry

        lax.fori_loop(0, ntiles // ATT_UNROLL, tiles, 0)

    def finish(it, carry):
        rows = pl.ds(pl.multiple_of(it * ATT_OUT_ROWS, ATT_OUT_ROWS), ATT_OUT_ROWS)
        o_ref[0, rows, :] = (acc[rows, :] / lrun[rows, :]).astype(o_ref.dtype)
        return carry

    lax.fori_loop(0, seq // ATT_OUT_ROWS, finish, 0)


def _dilated_mixture(qkv, rel_bias):
    b, s, _ = qkv.shape
    h, e = HEADS_A, HEAD_DIM
    dilations = _branch_dilations()
    for window, d in A_BRANCHES:
        assert window // (2 * d) == ATT_HALO and (s // d) % ATT_TQ == 0
    assert (s // ATT_TQ) % ATT_UNROLL == 0 and s % ATT_OUT_ROWS == 0
    chunk = min(ATT_CHUNK, s)
    assert s % chunk == 0 and chunk % ATT_OUT_ROWS == 0 and all((chunk // d) % 16 == 0 for d in dilations)
    win = ATT_TQ + 2 * ATT_HALO
    pad_rows = s + 2 * ATT_HALO * max(dilations)

    def in_spec(which):
        return pl.BlockSpec((1, s, e), lambda hi, bi: (bi, 0, which * h + hi))

    return pl.pallas_call(
        functools.partial(_attn_kernel, scale=1.0 / math.sqrt(e), dilations=dilations),
        grid=(h, b),
        in_specs=[pl.BlockSpec((1, len(dilations), 4, ATT_TQ, win), lambda hi, bi: (hi, 0, 0, 0, 0)),
                  in_spec(0), in_spec(1), in_spec(2)],
        out_specs=pl.BlockSpec((1, s, e), lambda hi, bi: (bi, 0, hi)),
        out_shape=jax.ShapeDtypeStruct((b, s, h * e), _BF16),
        scratch_shapes=[pltpu.VMEM((chunk, e), _F32), pltpu.VMEM((s, e), _BF16),
                        pltpu.VMEM((pad_rows, e), _BF16), pltpu.VMEM((pad_rows, e), _BF16),
                        pltpu.VMEM((s, e), _F32), pltpu.VMEM((s, LANES), _F32), pltpu.VMEM((s, LANES), _F32)],
        compiler_params=_params(("parallel", "parallel"), 56),
        name="dilated_attention",
    )(_attn_bias_tiles(rel_bias), qkv, qkv, qkv)


FFT_COLS = 512
FFT_K1 = 16


def _split(x):
    hi = x.astype(_BF16)
    return hi, (x - hi.astype(_F32)).astype(_BF16)


def _dot3(a, b):
    dot = functools.partial(jnp.dot, preferred_element_type=_F32)
    return dot(a[0], b[0]) + dot(a[1], b[0]) + dot(a[0], b[1])


def _fft_stage1_kernel(fh_ref, fl_ref, x_ref, ar_ref, ai_ref):
    n1, sub, width = x_ref.shape
    r = _dot3((fh_ref[...], fl_ref[...]), _split(x_ref[...].reshape(n1 * sub, width)))
    ar_ref[...] = r[:n1 * sub].reshape(n1, sub, width)
    ai_ref[...] = r[n1 * sub:].reshape(n1, sub, width)


def _fft_stage2_kernel(twc_ref, tws_ref, c2h, c2l, s2h, s2l, cch, ccl, sch, scl, ar_ref, ai_ref, o_ref, rows_f32):
    n2, gd = c2h.shape[0], cch.shape[0]
    c2, s2 = (c2h[...], c2l[...]), (s2h[...], s2l[...])
    cc, sc = (cch[...], ccl[...]), (sch[...], scl[...])
    for j in range(o_ref.shape[1]):
        rows = slice(j * n2, (j + 1) * n2)
        ar, ai = ar_ref[rows, :], ai_ref[rows, :]
        c, s = twc_ref[rows, :], tws_ref[rows, :]
        pr, pi = _split(ar * c + ai * s), _split(ai * c - ar * s)
        zr = _dot3(c2, pr) + _dot3(s2, pi)
        zi = _dot3(c2, pi) - _dot3(s2, pr)
        for g in range(ar.shape[1] // gd):
            cols = slice(g * gd, (g + 1) * gd)
            rows_f32[:, j, cols] = _dot3(_split(zr[:, cols]), cc) + _dot3(_split(zi[:, cols]), sc)
    o_ref[...] = rows_f32[...].astype(o_ref.dtype)


def _fourier_mix(u):
    b, s, width = u.shape
    gd = FNET_GROUP_DIM
    n2 = 128
    n1 = s // n2
    nk = min(FFT_K1, n1)
    cols = min(FFT_COLS, width)
    assert n1 * n2 == s and n1 % nk == 0 and nk % 16 == 0 and width % cols == 0 and cols % gd == 0

    def trig(n, rows, cols_):
        ang = 2.0 * np.pi * ((np.arange(rows)[:, None] * np.arange(cols_)[None, :]) % n) / n
        return np.cos(ang), np.sin(ang)

    def pair(x):
        return _split(jnp.asarray(x, _F32))

    c1, s1 = trig(n1, n1, n1)
    f1h, f1l = pair(np.kron(np.concatenate([c1, -s1], axis=0), np.eye(SUBLANES)))
    tc, tsn = trig(s, n1, n2)
    twc = jnp.asarray(tc.reshape(s, 1), _F32)
    tws = jnp.asarray(tsn.reshape(s, 1), _F32)
    c2, s2 = trig(n2, n2, n2)
    norm = 1.0 / math.sqrt(s * gd)
    cc, sc = trig(gd, gd, gd)
    consts = [*pair(c2), *pair(s2), *pair(cc * norm), *pair(sc * norm)]

    x_spec = pl.BlockSpec((None, n1, SUBLANES, width), lambda bi, j: (bi, 0, j, 0))
    ar, ai = pl.pallas_call(
        _fft_stage1_kernel,
        grid=(b, n2 // SUBLANES),
        in_specs=[pl.BlockSpec((2 * n1 * SUBLANES, n1 * SUBLANES), lambda bi, j: (0, 0))] * 2 + [x_spec],
        out_specs=[x_spec, x_spec],
        out_shape=[jax.ShapeDtypeStruct((b, n1, n2, width), _F32)] * 2,
        compiler_params=_params(("parallel", "parallel"), 40),
        name="fft_stage1",
    )(f1h, f1l, u.reshape(b, n1, n2, width))

    tw_spec = pl.BlockSpec((nk * n2, 1), lambda bi, i, c: (i, 0))
    m_spec = pl.BlockSpec((n2, n2), lambda bi, i, c: (0, 0))
    g_spec = pl.BlockSpec((gd, gd), lambda bi, i, c: (0, 0))
    z_spec = pl.BlockSpec((None, nk * n2, cols), lambda bi, i, c: (bi, i, c))
    out = pl.pallas_call(
        _fft_stage2_kernel,
        grid=(b, n1 // nk, width // cols),
        in_specs=[tw_spec, tw_spec] + [m_spec] * 4 + [g_spec] * 4 + [z_spec, z_spec],
        out_specs=pl.BlockSpec((None, n2, nk, cols), lambda bi, i, c: (bi, 0, i, c)),
        out_shape=jax.ShapeDtypeStruct((b, n2, n1, width), _BF16),
        scratch_shapes=[pltpu.VMEM((n2, nk, cols), _F32)],
        compiler_params=_params(("parallel", "parallel", "parallel"), 48),
        name="fft_stage2",
    )(twc, tws, *consts, ar.reshape(b, s, width), ai.reshape(b, s, width))
    return out.reshape(b, s, width)


CONV_HALO = 16
CONV_ROWS = 64


def _glu_conv_kernel(w_ref, b_ref, vc_ref, vp_ref, vn_ref, gc_ref, gp_ref, gn_ref, y_ref, ext, shifted, *, width):
    i, last = pl.program_id(1), pl.num_programs(1) - 1
    ts, tc = vc_ref.shape[1], vc_ref.shape[2]
    halo, pad = CONV_HALO, (width - 1) // 2

    def glu(v_ref, g_ref):
        return v_ref[0] * jax.nn.sigmoid(g_ref[0])

    ext[0:halo, :] = jnp.where(i > 0, glu(vp_ref, gp_ref), 0.0)
    ext[halo:halo + ts, :] = glu(vc_ref, gc_ref)
    ext[halo + ts:, :] = jnp.where(i < last, glu(vn_ref, gn_ref), 0.0)
    rows = min(CONV_ROWS, ts)
    span = shifted.shape[1]
    for cb in range(tc // LANES):
        cols = slice(cb * LANES, (cb + 1) * LANES)
        for k in range(1, SUBLANES):
            shifted[k - 1, :, cols] = ext[k:k + span, cols]
        for r0 in range(0, ts, rows):
            acc = jnp.broadcast_to(b_ref[:, cols], (rows, LANES))
            for t in range(width):
                start = r0 + halo - pad + t
                k, base = start % SUBLANES, start - start % SUBLANES
                tap = ext[base:base + rows, cols] if k == 0 else shifted[k - 1, base:base + rows, cols]
                acc = acc + w_ref[t:t + 1, cols] * tap
            y_ref[0, r0:r0 + rows, cols] = acc


def _glu_conv(proj, conv_w, conv_b):
    b, s, _ = proj.shape
    width, cdim = conv_w.shape
    assert (width - 1) // 2 <= CONV_HALO
    ts = _pick(s, (256, 128))
    tc = _pick(cdim, (512, 256, 128))
    ncb, hb = cdim // tc, ts // CONV_HALO
    nhalo = s // CONV_HALO
    w_pad = jnp.pad(conv_w, ((0, -width % 8), (0, 0)))

    def cur(off):
        return pl.BlockSpec((1, ts, tc), lambda bi, i, c: (bi, i, off * ncb + c))

    def prev(off):
        return pl.BlockSpec((1, CONV_HALO, tc), lambda bi, i, c: (bi, jnp.maximum(i * hb - 1, 0), off * ncb + c))

    def nxt(off):
        return pl.BlockSpec((1, CONV_HALO, tc),
                            lambda bi, i, c: (bi, jnp.minimum((i + 1) * hb, nhalo - 1), off * ncb + c))

    return pl.pallas_call(
        functools.partial(_glu_conv_kernel, width=width),
        grid=(b, s // ts, ncb),
        in_specs=[pl.BlockSpec((w_pad.shape[0], tc), lambda bi, i, c: (0, c)),
                  pl.BlockSpec((1, tc), lambda bi, i, c: (0, c)),
                  cur(0), prev(0), nxt(0), cur(1), prev(1), nxt(1)],
        out_specs=pl.BlockSpec((1, ts, tc), lambda bi, i, c: (bi, i, c)),
        out_shape=jax.ShapeDtypeStruct((b, s, cdim), _F32),
        scratch_shapes=[pltpu.VMEM((ts + 2 * CONV_HALO, tc), _F32),
                        pltpu.VMEM((SUBLANES - 1, ts + 2 * CONV_HALO - SUBLANES, tc), _F32)],
        compiler_params=_params(("parallel", "parallel", "parallel"), 32),
        name="glu_conv",
    )(w_pad, conv_b.reshape(1, cdim), proj, proj, proj, proj, proj, proj)


SHORT_HALO = 8


def _norm_gate_kernel(y_ref, lg_ref, lb_ref, w_ref, db_ref, cc_ref, cp_ref, cn_ref, hc_ref, hp_ref, hn_ref,
                      o_ref, ext, *, width):
    i, last = pl.program_id(1), pl.num_programs(1) - 1
    ts, cdim = y_ref.shape[1], y_ref.shape[2]
    ddim = db_ref.shape[2]
    halo, pad = SHORT_HALO, (width - 1) // 2
    rows = min(CONV_ROWS, ts)

    for r0 in range(0, ts, rows):
        y = y_ref[0, r0:r0 + rows, :]
        yc = y - jnp.mean(y, axis=-1, keepdims=True)
        inv = lax.rsqrt(jnp.mean(yc * yc, axis=-1, keepdims=True) + NORM_EPS)
        n = yc * inv * lg_ref[...] + lb_ref[...]
        o_ref[0, r0:r0 + rows, 0:cdim] = (n * jax.nn.sigmoid(n)).astype(o_ref.dtype)

    ext[0:halo, :] = jnp.where(i > 0, cp_ref[0] * hp_ref[0], 0.0)
    ext[halo:halo + ts, :] = cc_ref[0] * hc_ref[0]
    ext[halo + ts:, :] = jnp.where(i < last, cn_ref[0] * hn_ref[0], 0.0)
    for cb in range(ddim // LANES):
        cols = slice(cb * LANES, (cb + 1) * LANES)
        for r0 in range(0, ts, rows):
            acc = jnp.zeros((rows, LANES), _F32)
            for t in range(width):
                start = r0 + halo - pad + t
                acc = acc + w_ref[t:t + 1, cols] * ext[start:start + rows, cols]
            o_ref[0, r0:r0 + rows, cdim + cb * LANES:cdim + (cb + 1) * LANES] = (
                db_ref[0, r0:r0 + rows, cols] * acc).astype(o_ref.dtype)


def _norm_gate(y, proj, ln_g, ln_b, conv_d_w):
    b, s, cdim = y.shape
    width, ddim = conv_d_w.shape
    assert (width - 1) // 2 <= SHORT_HALO and cdim % ddim == 0
    ts = _pick(s, (256, 128))
    hb, nhalo = ts // SHORT_HALO, s // SHORT_HALO
    base = 2 * cdim // ddim
    w_pad = jnp.pad(conv_d_w, ((0, -width % 8), (0, 0)))

    def cur(off):
        return pl.BlockSpec((1, ts, ddim), lambda bi, i: (bi, i, base + off))

    def prev(off):
        return pl.BlockSpec((1, SHORT_HALO, ddim), lambda bi, i: (bi, jnp.maximum(i * hb - 1, 0), base + off))

    def nxt(off):
        return pl.BlockSpec((1, SHORT_HALO, ddim),
                            lambda bi, i: (bi, jnp.minimum((i + 1) * hb, nhalo - 1), base + off))

    vec = pl.BlockSpec((1, cdim), lambda bi, i: (0, 0))
    return pl.pallas_call(
        functools.partial(_norm_gate_kernel, width=width),
        grid=(b, s // ts),
        in_specs=[pl.BlockSpec((1, ts, cdim), lambda bi, i: (bi, i, 0)), vec, vec,
                  pl.BlockSpec((w_pad.shape[0], ddim), lambda bi, i: (0, 0)),
                  cur(0), cur(1), prev(1), nxt(1), cur(2), prev(2), nxt(2)],
        out_specs=pl.BlockSpec((1, ts, cdim + ddim), lambda bi, i: (bi, i, 0)),
        out_shape=jax.ShapeDtypeStruct((b, s, cdim + ddim), _BF16),
        scratch_shapes=[pltpu.VMEM((ts + 2 * SHORT_HALO, ddim), _F32)],
        compiler_params=_params(("parallel", "parallel"), 48),
        name="norm_gate",
    )(y, ln_g.reshape(1, cdim), ln_b.reshape(1, cdim), w_pad, proj, proj, proj, proj, proj, proj, proj)


def _mixer_ab(x, h, stats, w_in, w_out, index, rel_bias, b, s, next_gain):
    qkv_dim = HEADS_A * HEAD_DIM
    qkv = _scaled_matmul(h, stats, w_in, index, 0, 3 * qkv_dim, _BF16, "in_ab_qkv")
    u = _scaled_matmul(h, stats, w_in, index, 3 * qkv_dim, w_in.shape[2] - 3 * qkv_dim, _F32, "in_ab_u")
    a_out = _dilated_mixture(qkv.reshape(b, s, 3 * qkv_dim), rel_bias).reshape(b * s, qkv_dim)
    b_out = _fourier_mix(u.reshape(b, s, -1)).reshape(b * s, -1)
    return _matmul_resid([a_out, b_out], w_out[index].astype(_BF16), x, 1.0, "out_ab", gain=next_gain)


def _mixer_cd(x, h, stats, w_in, conv_c_w, conv_c_b, ln_c_g, ln_c_b, conv_d_w, w_out, index, b, s, next_gain):
    proj = _scaled_matmul(h, stats, w_in, index, 0, w_in.shape[2], _F32, "in_cd").reshape(b, s, -1)
    y = _glu_conv(proj, conv_c_w[index], conv_c_b[index])
    mixed = _norm_gate(y, proj, ln_c_g[index], ln_c_b[index], conv_d_w[index]).reshape(b * s, -1)
    return _matmul_resid([mixed], w_out[index].astype(_BF16), x, 1.0, "out_cd", gain=next_gain)


def kernel(x, ffn1_norm, ffn1_w_gate, ffn1_w_up, ffn1_w_down, mix_norm, ffn2_norm, ffn2_w_gate, ffn2_w_up,
           ffn2_w_down, rel_bias, w_in_ab, w_out_ab, w_in_cd, conv_c_w, conv_c_b, ln_c_g, ln_c_b, conv_d_w,
           w_out_cd, final_norm):
    b, s, d = x.shape
    depth = ffn1_norm.shape[0]
    x = x.reshape(b * s, d)
    h, stats = _rmsnorm(x, ffn1_norm[0], _BF16, "ffn1_l0_norm"), None
    for layer in range(depth):
        x, h, stats = _ffn(x, h, stats, ffn1_w_gate, ffn1_w_up, ffn1_w_down, layer, f"ffn1_l{layer}",
                           mix_norm[layer])
        i = layer // 2
        if layer % 2 == 0:
            x, h, stats = _mixer_ab(x, h, stats, w_in_ab, w_out_ab, i, rel_bias, b, s, ffn2_norm[layer])
        else:
            x, h, stats = _mixer_cd(x, h, stats, w_in_cd, conv_c_w, conv_c_b, ln_c_g, ln_c_b, conv_d_w,
                                    w_out_cd, i, b, s, ffn2_norm[layer])
        if layer + 1 < depth:
            x, h, stats = _ffn(x, h, stats, ffn2_w_gate, ffn2_w_up, ffn2_w_down, layer, f"ffn2_l{layer}",
                               ffn1_norm[layer + 1])
        else:
            x = _ffn(x, h, stats, ffn2_w_gate, ffn2_w_up, ffn2_w_down, layer, f"ffn2_l{layer}", None)
    return _rmsnorm(x, final_norm, _F32, "final_norm").reshape(b, s, d)
```
